```python
import jax, jax.numpy as jnp
from jax import lax
import numpy as np

D_MODEL = 1024
BATCH = 2
SEQ = 16384
DEPTH = 1

CHUNK = 64
D_MIX = D_MODEL
D_CONV = D_MIX // 2
D_GMLP = D_MIX - D_CONV
CONV_WIDTH = 31
GMLP_HEADS = 8
GMLP_HEAD_DIM = D_GMLP // GMLP_HEADS
GMLP_BLOCK = 128
N_EXPERTS = 32
TOP_K = 4
D_FF = D_MODEL
SWIGLU_ALPHA = 1.702
SWIGLU_LIMIT = 7.0
EXPERT_BLOCK = 256
EPS = 1e-5

kernel_name = "hybrid_conv_gmlp_moe_block"


def rms_norm(x, g):
    xf = x.astype(jnp.float32)
    y = xf * lax.rsqrt(jnp.mean(xf * xf, axis=-1, keepdims=True) + EPS)
    return (y * g.astype(jnp.float32)).astype(x.dtype)


def layer_norm(x, g, b):
    xf = x.astype(jnp.float32)
    mu = jnp.mean(xf, axis=-1, keepdims=True)
    xc = xf - mu
    var = jnp.mean(xc * xc, axis=-1, keepdims=True)
    y = xc * lax.rsqrt(var + EPS) * g.astype(jnp.float32) + b.astype(jnp.float32)
    return y.astype(x.dtype)


def conv_module(a, gate, w_dw, b_dw, ln_g, ln_b):
    u = a * jax.nn.sigmoid(gate)
    y = lax.conv_general_dilated(
        u, w_dw.astype(u.dtype)[:, None, :],
        window_strides=(1,), padding=[(CONV_WIDTH - 1, 0)],
        dimension_numbers=("NWC", "WIO", "NWC"),
        feature_group_count=D_CONV)
    y = y + b_dw
    y = layer_norm(y, ln_g, ln_b)
    return jax.nn.silu(y)


def spatial_gating(u, v, w_s, b_s, ln_g, ln_b):
    bsz, seq, _ = u.shape
    u = jax.nn.gelu(u, approximate=False)
    v = layer_norm(jax.nn.gelu(v, approximate=False), ln_g, ln_b)
    n_blk = seq // GMLP_BLOCK
    vb = v.reshape(bsz, n_blk, GMLP_BLOCK, GMLP_HEADS, GMLP_HEAD_DIM)
    t = jnp.arange(GMLP_BLOCK)
    mask = (t[None, :] // CHUNK) <= (t[:, None] // CHUNK)
    w = jnp.where(mask[None], w_s, 0.0).astype(v.dtype)
    mixed = jnp.einsum("hts,bnshd->bnthd", w, vb) + b_s.T[:, :, None].astype(v.dtype)
    return u * mixed.reshape(bsz, seq, D_GMLP)


def expert_swiglu(x, w1, b1, w2, b2):
    hcat = x @ w1 + b1
    x_glu = jnp.minimum(hcat[:, :D_FF], SWIGLU_LIMIT)
    x_lin = jnp.clip(hcat[:, D_FF:], -SWIGLU_LIMIT, SWIGLU_LIMIT)
    act = x_glu * jax.nn.sigmoid(SWIGLU_ALPHA * x_glu) * (x_lin + 1.0)
    return act @ w2 + b2


def moe_ffn(h, w_r, b_r, w1, b1, w2, b2):
    bsz, seq, d = h.shape
    n_tok = bsz * seq
    hf = h.reshape(n_tok, d)
    logits = (hf @ w_r + b_r).astype(jnp.float32)
    top_val, top_idx = lax.top_k(logits, TOP_K)
    gates = jax.nn.softmax(top_val, axis=-1)

    n_assign = n_tok * TOP_K
    flat_e = top_idx.reshape(-1).astype(jnp.int32)
    flat_tok = jnp.repeat(jnp.arange(n_tok, dtype=jnp.int32), TOP_K)
    flat_gate = gates.reshape(-1)
    order = jnp.argsort(flat_e, stable=True)
    e_sorted = flat_e[order]
    counts = jnp.bincount(flat_e, length=N_EXPERTS).astype(jnp.int32)
    blocks_per_e = (counts + EXPERT_BLOCK - 1) // EXPERT_BLOCK
    block_end = jnp.cumsum(blocks_per_e)
    pad_start = (block_end - blocks_per_e) * EXPERT_BLOCK
    start = jnp.cumsum(counts) - counts
    rank = jnp.arange(n_assign, dtype=jnp.int32) - start[e_sorted]
    dest = pad_start[e_sorted] + rank

    n_blocks = -(-n_assign // EXPERT_BLOCK) + N_EXPERTS
    n_slots = n_blocks * EXPERT_BLOCK
    slot_tok = jnp.full((n_slots,), n_tok, jnp.int32).at[dest].set(flat_tok[order])
    slot_gate = jnp.zeros((n_slots,), jnp.float32).at[dest].set(flat_gate[order])
    block_e = jnp.minimum(
        jnp.searchsorted(block_end, jnp.arange(n_blocks, dtype=jnp.int32), side="right"),
        N_EXPERTS - 1)
    h_pad = jnp.concatenate([hf, jnp.zeros((1, d), hf.dtype)], axis=0)

    def run_block(args):
        tok, gate, e = args
        y = expert_swiglu(h_pad[tok], w1[e], b1[e], w2[e], b2[e])
        return y * gate[:, None].astype(y.dtype)

    y = lax.map(run_block, (slot_tok.reshape(n_blocks, EXPERT_BLOCK),
                            slot_gate.reshape(n_blocks, EXPERT_BLOCK), block_e))
    out = jnp.zeros((n_tok + 1, d), y.dtype).at[slot_tok].add(y.reshape(n_slots, d))
    return out[:n_tok].reshape(bsz, seq, d).astype(h.dtype)


def setup_inputs(seed: int = 0) -> dict:
    key = jax.random.key(seed)
    ks = jax.random.split(key, 24)
    f32 = jnp.float32
    nrm = lambda k, shape, s: jax.random.normal(k, shape, f32) * s
    L = DEPTH
    return {
        "x": nrm(ks[0], (BATCH, SEQ, D_MODEL), 1.0),
        "norm1_g": 1.0 + nrm(ks[1], (L, D_MODEL), 0.02),
        "w_in": nrm(ks[2], (L, D_MODEL, 2 * D_CONV + 2 * D_GMLP), D_MODEL ** -0.5),
        "b_in": nrm(ks[3], (L, 2 * D_CONV + 2 * D_GMLP), 0.02),
        "conv_w": nrm(ks[4], (L, CONV_WIDTH, D_CONV), CONV_WIDTH ** -0.5),
        "conv_b": nrm(ks[5], (L, D_CONV), 0.02),
        "conv_ln_g": 1.0 + nrm(ks[6], (L, D_CONV), 0.02),
        "conv_ln_b": nrm(ks[7], (L, D_CONV), 0.02),
        "sg_ln_g": 1.0 + nrm(ks[8], (L, D_GMLP), 0.02),
        "sg_ln_b": nrm(ks[9], (L, D_GMLP), 0.02),
        "sg_w": nrm(ks[10], (L, GMLP_HEADS, GMLP_BLOCK, GMLP_BLOCK), GMLP_BLOCK ** -0.5),
        "sg_b": 1.0 + nrm(ks[11], (L, GMLP_HEADS, GMLP_BLOCK), 0.1),
        "grp_g_conv": 1.0 + nrm(ks[12], (L, D_CONV), 0.02),
        "grp_g_sg": 1.0 + nrm(ks[13], (L, D_GMLP), 0.02),
        "w_out": nrm(ks[14], (L, D_MIX, D_MODEL), D_MIX ** -0.5),
        "b_out": nrm(ks[15], (L, D_MODEL), 0.02),
        "norm2_g": 1.0 + nrm(ks[16], (L, D_MODEL), 0.02),
        "w_router": nrm(ks[17], (L, D_MODEL, N_EXPERTS), D_MODEL ** -0.5),
        "b_router": nrm(ks[18], (L, N_EXPERTS), 0.01),
        "w_exp1": nrm(ks[19], (L, N_EXPERTS, D_MODEL, 2 * D_FF), D_MODEL ** -0.5),
        "b_exp1": nrm(ks[20], (L, N_EXPERTS, 2 * D_FF), 0.02),
        "w_exp2": nrm(ks[21], (L, N_EXPERTS, D_FF, D_MODEL), D_FF ** -0.5),
        "b_exp2": nrm(ks[22], (L, N_EXPERTS, D_MODEL), 0.02),
        "final_g": 1.0 + nrm(ks[23], (D_MODEL,), 0.02),
    }


def reference(x, norm1_g, w_in, b_in, conv_w, conv_b, conv_ln_g, conv_ln_b, sg_ln_g, sg_ln_b,
              sg_w, sg_b, grp_g_conv, grp_g_sg, w_out, b_out, norm2_g, w_router, b_router,
              w_exp1, b_exp1, w_exp2, b_exp2, final_g):
    splits = [D_CONV, 2 * D_CONV, 2 * D_CONV + D_GMLP]
    for l in range(DEPTH):
        h = rms_norm(x, norm1_g[l])
        p = h @ w_in[l] + b_in[l]
        a, g, u, v = jnp.split(p, splits, axis=-1)
        y_conv = conv_module(a, g, conv_w[l], conv_b[l], conv_ln_g[l], conv_ln_b[l])
        y_sg = spatial_gating(u, v, sg_w[l], sg_b[l], sg_ln_g[l], sg_ln_b[l])
        y = jnp.concatenate([rms_norm(y_conv, grp_g_conv[l]), rms_norm(y_sg, grp_g_sg[l])], axis=-1)
        x = x + (y @ w_out[l] + b_out[l])
        h2 = rms_norm(x, norm2_g[l])
        x = x + moe_ffn(h2, w_router[l], b_router[l], w_exp1[l], b_exp1[l], w_exp2[l], b_exp2[l])
    return rms_norm(x, final_g)
```

```python
import functools

import jax
import jax.numpy as jnp
from jax import lax
from jax.experimental import pallas as pl
from jax.experimental.pallas import tpu as pltpu

D_MODEL = 1024
D_CONV = 512
D_GMLP = 512
CONV_WIDTH = 31
CONV_HIST = 32
GMLP_HEADS = 8
GMLP_HEAD_DIM = 64
GMLP_BLOCK = 128
CHUNK = 64
N_EXPERTS = 32
TOP_K = 4
D_FF = 1024
SWIGLU_ALPHA = 1.702
SWIGLU_LIMIT = 7.0
EPS = 1e-5

MIX_TILE = 512
CONV_ROWS = 32
EXPERT_BLOCK = 512
DISPATCH_TILE = 512
COMBINE_TILE = 256
VMEM_LIMIT = 56 * 1024 * 1024

F32 = jnp.float32
BF16 = jnp.bfloat16


def _rms(x, g):
    return x * lax.rsqrt(jnp.mean(x * x, axis=-1, keepdims=True) + EPS) * g


def _ln(x, g, b):
    mu = jnp.mean(x, axis=-1, keepdims=True)
    xc = x - mu
    var = jnp.mean(xc * xc, axis=-1, keepdims=True)
    return xc * lax.rsqrt(var + EPS) * g + b


def _gelu(x):
    return 0.5 * x * (1.0 + lax.erf(x * (2.0 ** -0.5)))


def _mix_kernel(x_ref, g1_ref, win_ref, bin_ref, cw_ref, cb_ref, clg_ref, clb_ref,
                slg_ref, slb_ref, sgw_ref, sgb_ref, ggc_ref, ggs_ref, wout_ref, bout_ref,
                g2_ref, wr_ref, br_ref,
                x1_ref, h2_ref, idx_ref, gate_ref, rank_ref, cnt_ref,
                ubuf, ycat, cnt_s):
    T = MIX_TILE
    b = pl.program_id(0)
    s = pl.program_id(1)

    @pl.when(s == 0)
    def _():
        ubuf[0:CONV_HIST, :] = jnp.zeros((CONV_HIST, D_CONV), F32)

    @pl.when((b == 0) & (s == 0))
    def _():
        cnt_s[...] = jnp.zeros_like(cnt_s)

    x = x_ref[...]
    h = _rms(x, g1_ref[...])
    p = jnp.dot(h.astype(BF16), win_ref[...], preferred_element_type=F32) + bin_ref[...]

    u = p[:, 0:D_CONV] * jax.nn.sigmoid(p[:, D_CONV:2 * D_CONV])
    ubuf[CONV_HIST:CONV_HIST + T, :] = u

    for c in range(T // CONV_ROWS):
        r0 = c * CONV_ROWS
        acc = jnp.broadcast_to(cb_ref[...], (CONV_ROWS, D_CONV))
        for k in range(CONV_WIDTH):
            off = r0 + CONV_HIST - (CONV_WIDTH - 1) + k
            acc = acc + cw_ref[k:k + 1, :] * ubuf[off:off + CONV_ROWS, :]
        y = _ln(acc, clg_ref[...], clb_ref[...])
        y = y * jax.nn.sigmoid(y)
        y = _rms(y, ggc_ref[...])
        ycat[r0:r0 + CONV_ROWS, 0:D_CONV] = y.astype(BF16)
    ubuf[0:CONV_HIST, :] = ubuf[T:T + CONV_HIST, :]

    uu = _gelu(p[:, 2 * D_CONV:2 * D_CONV + D_GMLP])
    vv = _ln(_gelu(p[:, 2 * D_CONV + D_GMLP:]), slg_ref[...], slb_ref[...])
    t_out = lax.broadcasted_iota(jnp.int32, (GMLP_BLOCK, 2 * GMLP_BLOCK), 0)
    s_in = lax.broadcasted_iota(jnp.int32, (GMLP_BLOCK, 2 * GMLP_BLOCK), 1) % GMLP_BLOCK
    wmask = (s_in // CHUNK) <= (t_out // CHUNK)
    lane = lax.broadcasted_iota(jnp.int32, (GMLP_BLOCK, 2 * GMLP_HEAD_DIM), 1)
    lo = lane < GMLP_HEAD_DIM
    n_pair = GMLP_HEADS // 2
    wcat = [jnp.where(wmask, sgw_ref[j], 0.0).astype(BF16) for j in range(n_pair)]
    for blk in range(T // GMLP_BLOCK):
        rows = slice(blk * GMLP_BLOCK, (blk + 1) * GMLP_BLOCK)
        outs = []
        for j in range(n_pair):
            cols = slice(j * 128, (j + 1) * 128)
            vp = vv[rows, cols]
            vstack = jnp.concatenate(
                [jnp.where(lo, vp, 0.0), jnp.where(lo, 0.0, vp)], axis=0).astype(BF16)
            mixed = jnp.dot(wcat[j], vstack, preferred_element_type=F32) + sgb_ref[:, cols]
            outs.append(uu[rows, cols] * mixed)
        ysg = jnp.concatenate(outs, axis=1)
        ycat[rows, D_CONV:] = _rms(ysg, ggs_ref[...]).astype(BF16)

    o = jnp.dot(ycat[...], wout_ref[...], preferred_element_type=F32) + bout_ref[...]
    x1 = x + o
    x1_ref[...] = x1
    h2 = _rms(x1, g2_ref[...])
    h2_ref[...] = h2

    logits = lax.dot_general(wr_ref[...].astype(BF16), h2.astype(BF16),
                             (((1,), (1,)), ((), ())),
                             preferred_element_type=F32) + br_ref[...]
    e_iota = lax.broadcasted_iota(jnp.int32, (N_EXPERTS, T), 0)
    work = logits
    vals, idxs, sels = [], [], []
    for _k in range(TOP_K):
        m = jnp.max(work, axis=0, keepdims=True)
        am = jnp.min(jnp.where(work == m, e_iota, N_EXPERTS), axis=0, keepdims=True)
        sel = e_iota == am
        vals.append(m)
        idxs.append(am)
        sels.append(sel)
        work = jnp.where(sel, -jnp.inf, work)
    exps = [jnp.exp(v - vals[0]) for v in vals]
    denom = exps[0] + exps[1] + exps[2] + exps[3]
    gate_ref[...] = jnp.concatenate([e / denom for e in exps], axis=0)
    idx_ref[...] = jnp.concatenate(idxs, axis=0)

    member = jnp.zeros((N_EXPERTS, T), F32)
    for sel in sels:
        member = member + jnp.where(sel, 1.0, 0.0)
    r_i = lax.broadcasted_iota(jnp.int32, (T, T), 0)
    c_i = lax.broadcasted_iota(jnp.int32, (T, T), 1)
    upper = jnp.where(r_i < c_i, 1.0, 0.0).astype(BF16)
    before = jnp.dot(member.astype(BF16), upper, preferred_element_type=F32)
    before = before + cnt_s[:, 0:1]
    ranks = [jnp.sum(jnp.where(sel, before, 0.0), axis=0, keepdims=True) for sel in sels]
    rank_ref[...] = jnp.concatenate(ranks, axis=0).astype(jnp.int32)
    cnt_new = cnt_s[...] + jnp.sum(member, axis=1, keepdims=True)
    cnt_s[...] = cnt_new
    cnt_ref[...] = cnt_new.astype(jnp.int32)


def _row_copy(src_ref, src_row, dst_ref, dst_row, sem):
    return pltpu.make_async_copy(src_ref.at[pl.ds(src_row, 1)], dst_ref.at[pl.ds(dst_row, 1)], sem)


def _dispatch_kernel(dest_ref, h2_ref, xs_in_ref, xs_ref, sem):
    del xs_in_ref
    base = pl.program_id(0) * DISPATCH_TILE

    def start(t, carry):
        for k in range(TOP_K):
            _row_copy(h2_ref, base + t, xs_ref, dest_ref[k, t], sem).start()
        return carry

    lax.fori_loop(0, DISPATCH_TILE, start, 0)

    def wait(t, carry):
        for k in range(TOP_K):
            _row_copy(h2_ref, base + t, xs_ref, dest_ref[k, t], sem).wait()
        return carry

    lax.fori_loop(0, DISPATCH_TILE, wait, 0)


def _expert_kernel(be_ref, tot_ref, xs_ref, w1_ref, b1_ref, w2_ref, b2_ref, y_ref):
    del be_ref

    @pl.when(pl.program_id(0) < tot_ref[0])
    def _():
        xb = xs_ref[...].astype(BF16)
        hcat = jnp.dot(xb, w1_ref[...], preferred_element_type=F32) + b1_ref[...]
        x_glu = jnp.minimum(hcat[:, :D_FF], SWIGLU_LIMIT)
        x_lin = jnp.clip(hcat[:, D_FF:], -SWIGLU_LIMIT, SWIGLU_LIMIT)
        act = x_glu * jax.nn.sigmoid(SWIGLU_ALPHA * x_glu) * (x_lin + 1.0)
        y_ref[...] = jnp.dot(act.astype(BF16), w2_ref[...], preferred_element_type=F32) + b2_ref[...]


def _combine_kernel(dest_ref, x1_ref, gate_ref, fg_ref, y_ref, out_ref, buf, sem):
    TC = COMBINE_TILE

    def start(t, carry):
        for k in range(TOP_K):
            pltpu.make_async_copy(y_ref.at[pl.ds(dest_ref[k, t], 1)],
                                  buf.at[k, pl.ds(t, 1)], sem).start()
        return carry

    lax.fori_loop(0, TC, start, 0)

    def wait(t, carry):
        for k in range(TOP_K):
            pltpu.make_async_copy(y_ref.at[pl.ds(dest_ref[k, t], 1)],
                                  buf.at[k, pl.ds(t, 1)], sem).wait()
        return carry

    lax.fori_loop(0, TC, wait, 0)

    gpad = jnp.concatenate([gate_ref[...], jnp.zeros((128 - TOP_K, TC), F32)], axis=0)
    gt = gpad.T
    acc = x1_ref[...]
    for k in range(TOP_K):
        acc = acc + gt[:, k:k + 1] * buf[k]
    out_ref[...] = _rms(acc, fg_ref[...])


def _full(shape):
    n = len(shape)
    return pl.BlockSpec(shape, lambda *_: (0,) * n)


def kernel(x, norm1_g, w_in, b_in, conv_w, conv_b, conv_ln_g, conv_ln_b, sg_ln_g, sg_ln_b,
           sg_w, sg_b, grp_g_conv, grp_g_sg, w_out, b_out, norm2_g, w_router, b_router,
           w_exp1, b_exp1, w_exp2, b_exp2, final_g):
    B, S, D = x.shape
    assert D == D_MODEL and S % MIX_TILE == 0 and norm1_g.shape[0] == 1
    N = B * S
    T = MIX_TILE
    n_s = S // T
    l = 0

    row = lambda v: v.reshape(1, -1).astype(F32)
    cw = jnp.concatenate([conv_w[l], jnp.zeros((1, D_CONV), F32)], axis=0)
    sgw = sg_w[l].reshape(GMLP_HEADS // 2, 2, GMLP_BLOCK, GMLP_BLOCK)
    sgw = sgw.transpose(0, 2, 1, 3).reshape(GMLP_HEADS // 2, GMLP_BLOCK, 2 * GMLP_BLOCK)
    sgb = jnp.repeat(sg_b[l].T, GMLP_HEAD_DIM, axis=1)
    wr_t = w_router[l].T
    br = b_router[l].reshape(N_EXPERTS, 1)

    mix_in = [
        x, row(norm1_g[l]), w_in[l].astype(BF16), row(b_in[l]), cw, row(conv_b[l]),
        row(conv_ln_g[l]), row(conv_ln_b[l]), row(sg_ln_g[l]), row(sg_ln_b[l]), sgw, sgb,
        row(grp_g_conv[l]), row(grp_g_sg[l]), w_out[l].astype(BF16), row(b_out[l]),
        row(norm2_g[l]), wr_t, br,
    ]
    mix_specs = [pl.BlockSpec((None, T, D), lambda b, s: (b, s, 0))]
    mix_specs += [_full(a.shape) for a in mix_in[1:]]
    tok_spec = pl.BlockSpec((TOP_K, T), lambda b, s: (0, b * n_s + s))
    x1, h2, idx, gate, rank, cnt = pl.pallas_call(
        _mix_kernel,
        grid=(B, n_s),
        in_specs=mix_specs,
        out_specs=[
            pl.BlockSpec((None, T, D), lambda b, s: (b, s, 0)),
            pl.BlockSpec((T, D), lambda b, s: (b * n_s + s, 0)),
            tok_spec, tok_spec, tok_spec,
            pl.BlockSpec((N_EXPERTS, 128), lambda b, s: (0, 0)),
        ],
        out_shape=[
            jax.ShapeDtypeStruct((B, S, D), F32),
            jax.ShapeDtypeStruct((N, D), F32),
            jax.ShapeDtypeStruct((TOP_K, N), jnp.int32),
            jax.ShapeDtypeStruct((TOP_K, N), F32),
            jax.ShapeDtypeStruct((TOP_K, N), jnp.int32),
            jax.ShapeDtypeStruct((N_EXPERTS, 128), jnp.int32),
        ],
        scratch_shapes=[
            pltpu.VMEM((T + CONV_HIST, D_CONV), F32),
            pltpu.VMEM((T, D_MODEL), BF16),
            pltpu.VMEM((N_EXPERTS, 128), F32),
        ],
        compiler_params=pltpu.CompilerParams(
            dimension_semantics=("arbitrary", "arbitrary"), vmem_limit_bytes=VMEM_LIMIT),
        name="mix",
    )(*mix_in)

    blk = EXPERT_BLOCK
    n_blocks = (N * TOP_K) // blk + N_EXPERTS
    n_slots = n_blocks * blk
    counts = cnt[:, 0]
    nblk_e = (counts + blk - 1) // blk
    block_end = jnp.cumsum(nblk_e)
    pad_start = (block_end - nblk_e) * blk
    dest = pad_start[idx] + rank
    block_e = jnp.minimum(
        jnp.searchsorted(block_end, jnp.arange(n_blocks, dtype=jnp.int32), side="right"),
        N_EXPERTS - 1).astype(jnp.int32)
    total = block_end[-1:].astype(jnp.int32)

    xs = pl.pallas_call(
        _dispatch_kernel,
        grid=(N // DISPATCH_TILE,),
        in_specs=[
            pl.BlockSpec((TOP_K, DISPATCH_TILE), lambda i: (0, i), memory_space=pltpu.SMEM),
            pl.BlockSpec(memory_space=pl.ANY),
            pl.BlockSpec(memory_space=pl.ANY),
        ],
        out_specs=pl.BlockSpec(memory_space=pl.ANY),
        out_shape=jax.ShapeDtypeStruct((n_slots, D), F32),
        scratch_shapes=[pltpu.SemaphoreType.DMA(())],
        input_output_aliases={2: 0},
        compiler_params=pltpu.CompilerParams(dimension_semantics=("arbitrary",)),
        name="dispatch",
    )(dest, h2, jnp.zeros((n_slots, D), F32))

    def blk_map(i, be, tot):
        return (jnp.minimum(i, tot[0] - 1), 0)

    def exp_map(i, be, tot):
        return (be[jnp.minimum(i, tot[0] - 1)], 0, 0)

    y = pl.pallas_call(
        _expert_kernel,
        grid_spec=pltpu.PrefetchScalarGridSpec(
            num_scalar_prefetch=2,
            grid=(n_blocks,),
            in_specs=[
                pl.BlockSpec((blk, D), blk_map),
                pl.BlockSpec((None, D, 2 * D_FF), exp_map),
                pl.BlockSpec((None, 1, 2 * D_FF), exp_map),
                pl.BlockSpec((None, D_FF, D), exp_map),
                pl.BlockSpec((None, 1, D), exp_map),
            ],
            out_specs=pl.BlockSpec((blk, D), blk_map),
        ),
        out_shape=jax.ShapeDtypeStruct((n_slots, D), F32),
        compiler_params=pltpu.CompilerParams(
            dimension_semantics=("arbitrary",), vmem_limit_bytes=VMEM_LIMIT),
        name="experts",
    )(block_e, total, xs, w_exp1[l].astype(BF16), b_exp1[l].reshape(N_EXPERTS, 1, 2 * D_FF),
      w_exp2[l].astype(BF16), b_exp2[l].reshape(N_EXPERTS, 1, D))

    TC = COMBINE_TILE
    out = pl.pallas_call(
        _combine_kernel,
        grid=(N // TC,),
        in_specs=[
            pl.BlockSpec((TOP_K, TC), lambda i: (0, i), memory_space=pltpu.SMEM),
            pl.BlockSpec((TC, D), lambda i: (i, 0)),
            pl.BlockSpec((TOP_K, TC), lambda i: (0, i)),
            _full((1, D)),
            pl.BlockSpec(memory_space=pl.ANY),
        ],
        out_specs=pl.BlockSpec((TC, D), lambda i: (i, 0)),
        out_shape=jax.ShapeDtypeStruct((N, D), F32),
        scratch_shapes=[pltpu.VMEM((TOP_K, TC, D), F32), pltpu.SemaphoreType.DMA(())],
        compiler_params=pltpu.CompilerParams(
            dimension_semantics=("arbitrary",), vmem_limit_bytes=VMEM_LIMIT),
        name="combine",
    )(dest, x1.reshape(N, D), gate, row(final_g), y)
    return out.reshape(B, S, D)
```

```python
import functools

import jax
import jax.numpy as jnp
from jax import lax
from jax.experimental import pallas as pl
from jax.experimental.pallas import tpu as pltpu

D_MODEL = 1024
D_CONV = 512
D_GMLP = 512
CONV_WIDTH = 31
CONV_HIST = 32
GMLP_HEADS = 8
GMLP_HEAD_DIM = 64
GMLP_BLOCK = 128
CHUNK = 64
N_EXPERTS = 32
TOP_K = 4
D_FF = 1024
SWIGLU_ALPHA = 1.702
SWIGLU_LIMIT = 7.0
EPS = 1e-5

MIX_TILE = 512
CONV_ROWS = 32
EXPERT_BLOCK = 512
DISPATCH_TILE = 512
COMBINE_TILE = 256
ISSUE_UNROLL = 8
VMEM_LIMIT = 56 * 1024 * 1024

F32 = jnp.float32
BF16 = jnp.bfloat16


def _rms(x, g):
    return x * lax.rsqrt(jnp.mean(x * x, axis=-1, keepdims=True) + EPS) * g


def _ln(x, g, b):
    mu = jnp.mean(x, axis=-1, keepdims=True)
    xc = x - mu
    var = jnp.mean(xc * xc, axis=-1, keepdims=True)
    return xc * lax.rsqrt(var + EPS) * g + b


def _gelu(x):
    return 0.5 * x * (1.0 + lax.erf(x * (2.0 ** -0.5)))


def _mix_kernel(x_ref, g1_ref, win_ref, bin_ref, cw_ref, cb_ref, clg_ref, clb_ref,
                slg_ref, slb_ref, sgw_ref, sgb_ref, ggc_ref, ggs_ref, wout_ref, bout_ref,
                g2_ref, wr_ref, br_ref,
                x1_ref, h2_ref, idx_ref, gate_ref, rank_ref, cnt_ref,
                ubuf, ycat, cnt_s):
    T = MIX_TILE
    b = pl.program_id(0)
    s = pl.program_id(1)

    @pl.when(s == 0)
    def _():
        ubuf[0:CONV_HIST, :] = jnp.zeros((CONV_HIST, D_CONV), F32)

    @pl.when((b == 0) & (s == 0))
    def _():
        cnt_s[...] = jnp.zeros_like(cnt_s)

    x = x_ref[...]
    h = _rms(x, g1_ref[...])
    p = jnp.dot(h.astype(BF16), win_ref[...], preferred_element_type=F32) + bin_ref[...]

    u = p[:, 0:D_CONV] * jax.nn.sigmoid(p[:, D_CONV:2 * D_CONV])
    ubuf[CONV_HIST:CONV_HIST + T, :] = u

    for c in range(T // CONV_ROWS):
        r0 = c * CONV_ROWS
        acc = jnp.broadcast_to(cb_ref[...], (CONV_ROWS, D_CONV))
        for k in range(CONV_WIDTH):
            off = r0 + CONV_HIST - (CONV_WIDTH - 1) + k
            acc = acc + cw_ref[k:k + 1, :] * ubuf[off:off + CONV_ROWS, :]
        y = _ln(acc, clg_ref[...], clb_ref[...])
        y = y * jax.nn.sigmoid(y)
        y = _rms(y, ggc_ref[...])
        ycat[r0:r0 + CONV_ROWS, 0:D_CONV] = y.astype(BF16)
    ubuf[0:CONV_HIST, :] = ubuf[T:T + CONV_HIST, :]

    uu = _gelu(p[:, 2 * D_CONV:2 * D_CONV + D_GMLP])
    vv = _ln(_gelu(p[:, 2 * D_CONV + D_GMLP:]), slg_ref[...], slb_ref[...])
    t_out = lax.broadcasted_iota(jnp.int32, (GMLP_BLOCK, 2 * GMLP_BLOCK), 0)
    s_in = lax.broadcasted_iota(jnp.int32, (GMLP_BLOCK, 2 * GMLP_BLOCK), 1) % GMLP_BLOCK
    wmask = (s_in // CHUNK) <= (t_out // CHUNK)
    lane = lax.broadcasted_iota(jnp.int32, (GMLP_BLOCK, 2 * GMLP_HEAD_DIM), 1)
    lo = lane < GMLP_HEAD_DIM
    n_pair = GMLP_HEADS // 2
    wcat = [jnp.where(wmask, sgw_ref[j], 0.0).astype(BF16) for j in range(n_pair)]
    for blk in range(T // GMLP_BLOCK):
        rows = slice(blk * GMLP_BLOCK, (blk + 1) * GMLP_BLOCK)
        outs = []
        for j in range(n_pair):
            cols = slice(j * 128, (j + 1) * 128)
            vp = vv[rows, cols]
            vstack = jnp.concatenate(
                [jnp.where(lo, vp, 0.0), jnp.where(lo, 0.0, vp)], axis=0).astype(BF16)
            mixed = jnp.dot(wcat[j], vstack, preferred_element_type=F32) + sgb_ref[:, cols]
            outs.append(uu[rows, cols] * mixed)
        ysg = jnp.concatenate(outs, axis=1)
        ycat[rows, D_CONV:] = _rms(ysg, ggs_ref[...]).astype(BF16)

    o = jnp.dot(ycat[...], wout_ref[...], preferred_element_type=F32) + bout_ref[...]
    x1 = x + o
    x1_ref[...] = x1
    h2 = _rms(x1, g2_ref[...])
    h2_ref[...] = h2

    logits = lax.dot_general(wr_ref[...].astype(BF16), h2.astype(BF16),
                             (((1,), (1,)), ((), ())),
                             preferred_element_type=F32) + br_ref[...]
    e_iota = lax.broadcasted_iota(jnp.int32, (N_EXPERTS, T), 0)
    work = logits
    vals, idxs, sels = [], [], []
    for _k in range(TOP_K):
        m = jnp.max(work, axis=0, keepdims=True)
        am = jnp.min(jnp.where(work == m, e_iota, N_EXPERTS), axis=0, keepdims=True)
        sel = e_iota == am
        vals.append(m)
        idxs.append(am)
        sels.append(sel)
        work = jnp.where(sel, -jnp.inf, work)
    exps = [jnp.exp(v - vals[0]) for v in vals]
    denom = exps[0] + exps[1] + exps[2] + exps[3]
    gate_ref[...] = jnp.concatenate([e / denom for e in exps], axis=0)
    idx_ref[...] = jnp.concatenate(idxs, axis=0)

    member = jnp.zeros((N_EXPERTS, T), F32)
    for sel in sels:
        member = member + jnp.where(sel, 1.0, 0.0)
    r_i = lax.broadcasted_iota(jnp.int32, (T, T), 0)
    c_i = lax.broadcasted_iota(jnp.int32, (T, T), 1)
    upper = jnp.where(r_i < c_i, 1.0, 0.0).astype(BF16)
    before = jnp.dot(member.astype(BF16), upper, preferred_element_type=F32)
    before = before + cnt_s[:, 0:1]
    ranks = [jnp.sum(jnp.where(sel, before, 0.0), axis=0, keepdims=True) for sel in sels]
    rank_ref[...] = jnp.concatenate(ranks, axis=0).astype(jnp.int32)
    cnt_new = cnt_s[...] + jnp.sum(member, axis=1, keepdims=True)
    cnt_s[...] = cnt_new
    cnt_ref[...] = cnt_new.astype(jnp.int32)


def _dispatch_kernel(dest_ref, h2_ref, xs_in_ref, xs_ref, sem):
    del xs_in_ref

    def start(g, carry):
        for j in range(ISSUE_UNROLL):
            t = g * ISSUE_UNROLL + j
            for k in range(TOP_K):
                pltpu.make_async_copy(h2_ref.at[pl.ds(t, 1)],
                                      xs_ref.at[pl.ds(dest_ref[k, t], 1)], sem).start()
        return carry

    lax.fori_loop(0, DISPATCH_TILE // ISSUE_UNROLL, start, 0)
    for k in range(TOP_K):
        pltpu.make_async_copy(h2_ref, xs_ref.at[pl.ds(0, DISPATCH_TILE)], sem).wait()


def _expert_kernel(be_ref, tot_ref, xs_ref, w1_ref, b1_ref, w2_ref, b2_ref, y_ref):
    del be_ref

    @pl.when(pl.program_id(0) < tot_ref[0])
    def _():
        xb = xs_ref[...].astype(BF16)
        hcat = jnp.dot(xb, w1_ref[...], preferred_element_type=F32) + b1_ref[...]
        x_glu = jnp.minimum(hcat[:, :D_FF], SWIGLU_LIMIT)
        x_lin = jnp.clip(hcat[:, D_FF:], -SWIGLU_LIMIT, SWIGLU_LIMIT)
        act = x_glu * jax.nn.sigmoid(SWIGLU_ALPHA * x_glu) * (x_lin + 1.0)
        y_ref[...] = jnp.dot(act.astype(BF16), w2_ref[...], preferred_element_type=F32) + b2_ref[...]

    @pl.when(pl.program_id(0) >= tot_ref[0])
    def _():
        y_ref[...] = jnp.zeros_like(y_ref)


def _combine_kernel(dest_ref, x1_ref, gate_ref, fg_ref, y_ref, out_ref, buf, sem):
    TC = COMBINE_TILE

    def start(g, carry):
        for j in range(ISSUE_UNROLL):
            t = g * ISSUE_UNROLL + j
            for k in range(TOP_K):
                pltpu.make_async_copy(y_ref.at[pl.ds(dest_ref[k, t], 1)],
                                      buf.at[k, pl.ds(t, 1)], sem).start()
        return carry

    lax.fori_loop(0, TC // ISSUE_UNROLL, start, 0)
    for k in range(TOP_K):
        pltpu.make_async_copy(y_ref.at[pl.ds(0, TC)], buf.at[k], sem).wait()

    gpad = jnp.concatenate([gate_ref[...], jnp.zeros((128 - TOP_K, TC), F32)], axis=0)
    gt = gpad.T
    acc = x1_ref[...]
    for k in range(TOP_K):
        acc = acc + gt[:, k:k + 1] * buf[k]
    out_ref[...] = _rms(acc, fg_ref[...])


def _full(shape):
    n = len(shape)
    return pl.BlockSpec(shape, lambda *_: (0,) * n)


def kernel(x, norm1_g, w_in, b_in, conv_w, conv_b, conv_ln_g, conv_ln_b, sg_ln_g, sg_ln_b,
           sg_w, sg_b, grp_g_conv, grp_g_sg, w_out, b_out, norm2_g, w_router, b_router,
           w_exp1, b_exp1, w_exp2, b_exp2, final_g):
    B, S, D = x.shape
    assert D == D_MODEL and S % MIX_TILE == 0 and norm1_g.shape[0] == 1
    N = B * S
    T = MIX_TILE
    n_s = S // T
    l = 0

    row = lambda v: v.reshape(1, -1).astype(F32)
    cw = jnp.concatenate([conv_w[l], jnp.zeros((1, D_CONV), F32)], axis=0)
    sgw = sg_w[l].reshape(GMLP_HEADS // 2, 2, GMLP_BLOCK, GMLP_BLOCK)
    sgw = sgw.transpose(0, 2, 1, 3).reshape(GMLP_HEADS // 2, GMLP_BLOCK, 2 * GMLP_BLOCK)
    sgb = jnp.repeat(sg_b[l].T, GMLP_HEAD_DIM, axis=1)
    wr_t = w_router[l].T
    br = b_router[l].reshape(N_EXPERTS, 1)

    mix_in = [
        x, row(norm1_g[l]), w_in[l].astype(BF16), row(b_in[l]), cw, row(conv_b[l]),
        row(conv_ln_g[l]), row(conv_ln_b[l]), row(sg_ln_g[l]), row(sg_ln_b[l]), sgw, sgb,
        row(grp_g_conv[l]), row(grp_g_sg[l]), w_out[l].astype(BF16), row(b_out[l]),
        row(norm2_g[l]), wr_t, br,
    ]
    mix_specs = [pl.BlockSpec((None, T, D), lambda b, s: (b, s, 0))]
    mix_specs += [_full(a.shape) for a in mix_in[1:]]
    tok_spec = pl.BlockSpec((TOP_K, T), lambda b, s: (0, b * n_s + s))
    x1, h2, idx, gate, rank, cnt = pl.pallas_call(
        _mix_kernel,
        grid=(B, n_s),
        in_specs=mix_specs,
        out_specs=[
            pl.BlockSpec((None, T, D), lambda b, s: (b, s, 0)),
            pl.BlockSpec((T, D), lambda b, s: (b * n_s + s, 0)),
            tok_spec, tok_spec, tok_spec,
            pl.BlockSpec((N_EXPERTS, 128), lambda b, s: (0, 0)),
        ],
        out_shape=[
            jax.ShapeDtypeStruct((B, S, D), F32),
            jax.ShapeDtypeStruct((N, D), F32),
            jax.ShapeDtypeStruct((TOP_K, N), jnp.int32),
            jax.ShapeDtypeStruct((TOP_K, N), F32),
            jax.ShapeDtypeStruct((TOP_K, N), jnp.int32),
            jax.ShapeDtypeStruct((N_EXPERTS, 128), jnp.int32),
        ],
        scratch_shapes=[
            pltpu.VMEM((T + CONV_HIST, D_CONV), F32),
            pltpu.VMEM((T, D_MODEL), BF16),
            pltpu.VMEM((N_EXPERTS, 128), F32),
        ],
        compiler_params=pltpu.CompilerParams(
            dimension_semantics=("arbitrary", "arbitrary"), vmem_limit_bytes=VMEM_LIMIT),
        name="mix",
    )(*mix_in)

    blk = EXPERT_BLOCK
    n_blocks = (N * TOP_K) // blk + N_EXPERTS
    n_slots = n_blocks * blk
    counts = cnt[:, 0]
    nblk_e = (counts + blk - 1) // blk
    block_end = jnp.cumsum(nblk_e)
    pad_start = (block_end - nblk_e) * blk
    dest = rank
    for e in range(N_EXPERTS):
        dest = dest + jnp.where(idx == e, pad_start[e], 0)
    blk_ids = jnp.arange(n_blocks, dtype=jnp.int32)
    block_e = jnp.minimum(
        jnp.sum((blk_ids[:, None] >= block_end[None, :]).astype(jnp.int32), axis=1),
        N_EXPERTS - 1).astype(jnp.int32)
    total = block_end[-1:].astype(jnp.int32)

    xs = pl.pallas_call(
        _dispatch_kernel,
        grid=(N // DISPATCH_TILE,),
        in_specs=[
            pl.BlockSpec((TOP_K, DISPATCH_TILE), lambda i: (0, i), memory_space=pltpu.SMEM),
            pl.BlockSpec((DISPATCH_TILE, D), lambda i: (i, 0)),
            pl.BlockSpec(memory_space=pl.ANY),
        ],
        out_specs=pl.BlockSpec(memory_space=pl.ANY),
        out_shape=jax.ShapeDtypeStruct((n_slots, D), F32),
        scratch_shapes=[pltpu.SemaphoreType.DMA(())],
        input_output_aliases={2: 0},
        compiler_params=pltpu.CompilerParams(dimension_semantics=("arbitrary",)),
        name="dispatch",
    )(dest, h2, jnp.zeros((n_slots, D), F32))

    def blk_map(i, be, tot):
        return (jnp.minimum(i, tot[0] - 1), 0)

    def exp_map(i, be, tot):
        return (be[jnp.minimum(i, tot[0] - 1)], 0, 0)

    y = pl.pallas_call(
        _expert_kernel,
        grid_spec=pltpu.PrefetchScalarGridSpec(
            num_scalar_prefetch=2,
            grid=(n_blocks,),
            in_specs=[
                pl.BlockSpec((blk, D), blk_map),
                pl.BlockSpec((None, D, 2 * D_FF), exp_map),
                pl.BlockSpec((None, 1, 2 * D_FF), exp_map),
                pl.BlockSpec((None, D_FF, D), exp_map),
                pl.BlockSpec((None, 1, D), exp_map),
            ],
            out_specs=pl.BlockSpec((blk, D), lambda i, be, tot: (i, 0)),
        ),
        out_shape=jax.ShapeDtypeStruct((n_slots, D), F32),
        compiler_params=pltpu.CompilerParams(
            dimension_semantics=("arbitrary",), vmem_limit_bytes=VMEM_LIMIT),
        name="experts",
    )(block_e, total, xs, w_exp1[l].astype(BF16), b_exp1[l].reshape(N_EXPERTS, 1, 2 * D_FF),
      w_exp2[l].astype(BF16), b_exp2[l].reshape(N_EXPERTS, 1, D))

    TC = COMBINE_TILE
    out = pl.pallas_call(
        _combine_kernel,
        grid=(N // TC,),
        in_specs=[
            pl.BlockSpec((TOP_K, TC), lambda i: (0, i), memory_space=pltpu.SMEM),
            pl.BlockSpec((TC, D), lambda i: (i, 0)),
            pl.BlockSpec((TOP_K, TC), lambda i: (0, i)),
            _full((1, D)),
            pl.BlockSpec(memory_space=pl.ANY),
        ],
        out_specs=pl.BlockSpec((TC, D), lambda i: (i, 0)),
        out_shape=jax.ShapeDtypeStruct((N, D), F32),
        scratch_shapes=[pltpu.VMEM((TOP_K, TC, D), F32), pltpu.SemaphoreType.DMA(())],
        compiler_params=pltpu.CompilerParams(
            dimension_semantics=("arbitrary",), vmem_limit_bytes=VMEM_LIMIT),
        name="combine",
    )(dest, x1.reshape(N, D), gate, row(final_g), y)
    return out.reshape(B, S, D)
```

```python
import functools

import jax
import jax.numpy as jnp
from jax import lax
from jax.experimental import pallas as pl
from jax.experimental.pallas import tpu as pltpu

D_MODEL = 1024
D_CONV = 512
D_GMLP = 512
CONV_WIDTH = 31
CONV_HIST = 32
GMLP_HEADS = 8
GMLP_HEAD_DIM = 64
GMLP_BLOCK = 128
CHUNK = 64
N_EXPERTS = 32
TOP_K = 4
D_FF = 1024
SWIGLU_ALPHA = 1.702
SWIGLU_LIMIT = 7.0
EPS = 1e-5

MIX_TILE = 512
CONV_ROWS = 32
EXPERT_BLOCK = 512
DISPATCH_TILE = 512
COMBINE_TILE = 256
ISSUE_UNROLL = 8
VMEM_LIMIT = 56 * 1024 * 1024

F32 = jnp.float32
BF16 = jnp.bfloat16


def _rms(x, g):
    return x * lax.rsqrt(jnp.mean(x * x, axis=-1, keepdims=True) + EPS) * g


def _ln(x, g, b):
    mu = jnp.mean(x, axis=-1, keepdims=True)
    xc = x - mu
    var = jnp.mean(xc * xc, axis=-1, keepdims=True)
    return xc * lax.rsqrt(var + EPS) * g + b


def _gelu(x):
    return 0.5 * x * (1.0 + lax.erf(x * (2.0 ** -0.5)))


LANES = 128
TOKEN_LINES = D_MODEL // LANES


def _load_token_tiles(ref, rows, lead=()):
    chunks = [ref[lead + (pl.ds(c, rows, stride=TOKEN_LINES), slice(None))] for c in range(TOKEN_LINES)]
    return jnp.concatenate(chunks, axis=1)


def _store_token_tiles(ref, value, rows):
    for c in range(TOKEN_LINES):
        ref[pl.ds(c, rows, stride=TOKEN_LINES), :] = value[:, c * LANES:(c + 1) * LANES]


def _mix_kernel(x_ref, g1_ref, win_ref, bin_ref, cw_ref, cb_ref, clg_ref, clb_ref,
                slg_ref, slb_ref, sgw_ref, sgb_ref, ggc_ref, ggs_ref, wout_ref, bout_ref,
                g2_ref, wr_ref, br_ref,
                x1_ref, h2_ref, idx_ref, gate_ref, rank_ref, cnt_ref,
                ubuf, ycat, cnt_s):
    T = MIX_TILE
    b = pl.program_id(0)
    s = pl.program_id(1)

    @pl.when(s == 0)
    def _():
        ubuf[0:CONV_HIST, :] = jnp.zeros((CONV_HIST, D_CONV), F32)

    @pl.when((b == 0) & (s == 0))
    def _():
        cnt_s[...] = jnp.zeros_like(cnt_s)

    x = x_ref[...]
    h = _rms(x, g1_ref[...])
    p = jnp.dot(h.astype(BF16), win_ref[...], preferred_element_type=F32) + bin_ref[...]

    u = p[:, 0:D_CONV] * jax.nn.sigmoid(p[:, D_CONV:2 * D_CONV])
    ubuf[CONV_HIST:CONV_HIST + T, :] = u

    for c in range(T // CONV_ROWS):
        r0 = c * CONV_ROWS
        acc = jnp.broadcast_to(cb_ref[...], (CONV_ROWS, D_CONV))
        for k in range(CONV_WIDTH):
            off = r0 + CONV_HIST - (CONV_WIDTH - 1) + k
            acc = acc + cw_ref[k:k + 1, :] * ubuf[off:off + CONV_ROWS, :]
        y = _ln(acc, clg_ref[...], clb_ref[...])
        y = y * jax.nn.sigmoid(y)
        y = _rms(y, ggc_ref[...])
        ycat[r0:r0 + CONV_ROWS, 0:D_CONV] = y.astype(BF16)
    ubuf[0:CONV_HIST, :] = ubuf[T:T + CONV_HIST, :]

    uu = _gelu(p[:, 2 * D_CONV:2 * D_CONV + D_GMLP])
    vv = _ln(_gelu(p[:, 2 * D_CONV + D_GMLP:]), slg_ref[...], slb_ref[...])
    t_out = lax.broadcasted_iota(jnp.int32, (GMLP_BLOCK, 2 * GMLP_BLOCK), 0)
    s_in = lax.broadcasted_iota(jnp.int32, (GMLP_BLOCK, 2 * GMLP_BLOCK), 1) % GMLP_BLOCK
    wmask = (s_in // CHUNK) <= (t_out // CHUNK)
    lane = lax.broadcasted_iota(jnp.int32, (GMLP_BLOCK, 2 * GMLP_HEAD_DIM), 1)
    lo = lane < GMLP_HEAD_DIM
    n_pair = GMLP_HEADS // 2
    wcat = [jnp.where(wmask, sgw_ref[j], 0.0).astype(BF16) for j in range(n_pair)]
    for blk in range(T // GMLP_BLOCK):
        rows = slice(blk * GMLP_BLOCK, (blk + 1) * GMLP_BLOCK)
        outs = []
        for j in range(n_pair):
            cols = slice(j * 128, (j + 1) * 128)
            vp = vv[rows, cols]
            vstack = jnp.concatenate(
                [jnp.where(lo, vp, 0.0), jnp.where(lo, 0.0, vp)], axis=0).astype(BF16)
            mixed = jnp.dot(wcat[j], vstack, preferred_element_type=F32) + sgb_ref[:, cols]
            outs.append(uu[rows, cols] * mixed)
        ysg = jnp.concatenate(outs, axis=1)
        ycat[rows, D_CONV:] = _rms(ysg, ggs_ref[...]).astype(BF16)

    o = jnp.dot(ycat[...], wout_ref[...], preferred_element_type=F32) + bout_ref[...]
    x1 = x + o
    x1_ref[...] = x1
    h2 = _rms(x1, g2_ref[...])
    _store_token_tiles(h2_ref, h2, T)

    logits = lax.dot_general(wr_ref[...].astype(BF16), h2.astype(BF16),
                             (((1,), (1,)), ((), ())),
                             preferred_element_type=F32) + br_ref[...]
    e_iota = lax.broadcasted_iota(jnp.int32, (N_EXPERTS, T), 0)
    work = logits
    vals, idxs, sels = [], [], []
    for _k in range(TOP_K):
        m = jnp.max(work, axis=0, keepdims=True)
        am = jnp.min(jnp.where(work == m, e_iota, N_EXPERTS), axis=0, keepdims=True)
        sel = e_iota == am
        vals.append(m)
        idxs.append(am)
        sels.append(sel)
        work = jnp.where(sel, -jnp.inf, work)
    exps = [jnp.exp(v - vals[0]) for v in vals]
    denom = exps[0] + exps[1] + exps[2] + exps[3]
    gate_ref[...] = jnp.concatenate([e / denom for e in exps], axis=0)
    idx_ref[...] = jnp.concatenate(idxs, axis=0)

    member = jnp.zeros((N_EXPERTS, T), F32)
    for sel in sels:
        member = member + jnp.where(sel, 1.0, 0.0)
    r_i = lax.broadcasted_iota(jnp.int32, (T, T), 0)
    c_i = lax.broadcasted_iota(jnp.int32, (T, T), 1)
    upper = jnp.where(r_i < c_i, 1.0, 0.0).astype(BF16)
    before = jnp.dot(member.astype(BF16), upper, preferred_element_type=F32)
    before = before + cnt_s[:, 0:1]
    ranks = [jnp.sum(jnp.where(sel, before, 0.0), axis=0, keepdims=True) for sel in sels]
    rank_ref[...] = jnp.concatenate(ranks, axis=0).astype(jnp.int32)
    cnt_new = cnt_s[...] + jnp.sum(member, axis=1, keepdims=True)
    cnt_s[...] = cnt_new
    cnt_ref[...] = cnt_new.astype(jnp.int32)


def _dispatch_kernel(dest_ref, h2_ref, xs_in_ref, xs_ref, sem):
    del xs_in_ref

    def start(g, carry):
        for j in range(ISSUE_UNROLL):
            t = g * ISSUE_UNROLL + j
            src = h2_ref.at[pl.ds(pl.multiple_of(t * TOKEN_LINES, TOKEN_LINES), TOKEN_LINES)]
            for k in range(TOP_K):
                line = pl.multiple_of(dest_ref[k, t], TOKEN_LINES)
                pltpu.make_async_copy(src, xs_ref.at[pl.ds(line, TOKEN_LINES)], sem).start(priority=k % 2)
        return carry

    lax.fori_loop(0, DISPATCH_TILE // ISSUE_UNROLL, start, 0)
    for k in range(TOP_K):
        pltpu.make_async_copy(h2_ref, xs_ref.at[pl.ds(0, DISPATCH_TILE * TOKEN_LINES)], sem).wait()


def _expert_kernel(be_ref, tot_ref, xs_ref, w1_ref, b1_ref, w2_ref, b2_ref, y_ref):
    del be_ref

    @pl.when(pl.program_id(0) < tot_ref[0])
    def _():
        xb = _load_token_tiles(xs_ref, EXPERT_BLOCK).astype(BF16)
        hcat = jnp.dot(xb, w1_ref[...], preferred_element_type=F32) + b1_ref[...]
        x_glu = jnp.minimum(hcat[:, :D_FF], SWIGLU_LIMIT)
        x_lin = jnp.clip(hcat[:, D_FF:], -SWIGLU_LIMIT, SWIGLU_LIMIT)
        act = x_glu * jax.nn.sigmoid(SWIGLU_ALPHA * x_glu) * (x_lin + 1.0)
        y = jnp.dot(act.astype(BF16), w2_ref[...], preferred_element_type=F32) + b2_ref[...]
        _store_token_tiles(y_ref, y, EXPERT_BLOCK)

    @pl.when(pl.program_id(0) >= tot_ref[0])
    def _():
        y_ref[...] = jnp.zeros_like(y_ref)


def _combine_kernel(dest_ref, x1_ref, gate_ref, fg_ref, y_ref, out_ref, buf, sem):
    TC = COMBINE_TILE

    def start(g, carry):
        for j in range(ISSUE_UNROLL):
            t = g * ISSUE_UNROLL + j
            row = pl.multiple_of(t * TOKEN_LINES, TOKEN_LINES)
            for k in range(TOP_K):
                line = pl.multiple_of(dest_ref[k, t], TOKEN_LINES)
                pltpu.make_async_copy(y_ref.at[pl.ds(line, TOKEN_LINES)],
                                      buf.at[k, pl.ds(row, TOKEN_LINES)], sem).start(priority=k % 2)
        return carry

    lax.fori_loop(0, TC // ISSUE_UNROLL, start, 0)
    for k in range(TOP_K):
        pltpu.make_async_copy(y_ref.at[pl.ds(0, TC * TOKEN_LINES)], buf.at[k], sem).wait()

    gpad = jnp.concatenate([gate_ref[...], jnp.zeros((128 - TOP_K, TC), F32)], axis=0)
    gt = gpad.T
    acc = x1_ref[...]
    for k in range(TOP_K):
        acc = acc + gt[:, k:k + 1] * _load_token_tiles(buf, TC, lead=(k,))
    out_ref[...] = _rms(acc, fg_ref[...])


def _full(shape):
    n = len(shape)
    return pl.BlockSpec(shape, lambda *_: (0,) * n)


def kernel(x, norm1_g, w_in, b_in, conv_w, conv_b, conv_ln_g, conv_ln_b, sg_ln_g, sg_ln_b,
           sg_w, sg_b, grp_g_conv, grp_g_sg, w_out, b_out, norm2_g, w_router, b_router,
           w_exp1, b_exp1, w_exp2, b_exp2, final_g):
    B, S, D = x.shape
    assert D == D_MODEL and S % MIX_TILE == 0 and norm1_g.shape[0] == 1
    N = B * S
    T = MIX_TILE
    n_s = S // T
    l = 0

    row = lambda v: v.reshape(1, -1).astype(F32)
    cw = jnp.concatenate([conv_w[l], jnp.zeros((1, D_CONV), F32)], axis=0)
    sgw = sg_w[l].reshape(GMLP_HEADS // 2, 2, GMLP_BLOCK, GMLP_BLOCK)
    sgw = sgw.transpose(0, 2, 1, 3).reshape(GMLP_HEADS // 2, GMLP_BLOCK, 2 * GMLP_BLOCK)
    sgb = jnp.repeat(sg_b[l].T, GMLP_HEAD_DIM, axis=1)
    wr_t = w_router[l].T
    br = b_router[l].reshape(N_EXPERTS, 1)

    mix_in = [
        x, row(norm1_g[l]), w_in[l].astype(BF16), row(b_in[l]), cw, row(conv_b[l]),
        row(conv_ln_g[l]), row(conv_ln_b[l]), row(sg_ln_g[l]), row(sg_ln_b[l]), sgw, sgb,
        row(grp_g_conv[l]), row(grp_g_sg[l]), w_out[l].astype(BF16), row(b_out[l]),
        row(norm2_g[l]), wr_t, br,
    ]
    mix_specs = [pl.BlockSpec((None, T, D), lambda b, s: (b, s, 0))]
    mix_specs += [_full(a.shape) for a in mix_in[1:]]
    tok_spec = pl.BlockSpec((TOP_K, T), lambda b, s: (0, b * n_s + s))
    x1, h2, idx, gate, rank, cnt = pl.pallas_call(
        _mix_kernel,
        grid=(B, n_s),
        in_specs=mix_specs,
        out_specs=[
            pl.BlockSpec((None, T, D), lambda b, s: (b, s, 0)),
            pl.BlockSpec((T * TOKEN_LINES, LANES), lambda b, s: (b * n_s + s, 0)),
            tok_spec, tok_spec, tok_spec,
            pl.BlockSpec((N_EXPERTS, 128), lambda b, s: (0, 0)),
        ],
        out_shape=[
            jax.ShapeDtypeStruct((B, S, D), F32),
            jax.ShapeDtypeStruct((N * TOKEN_LINES, LANES), F32),
            jax.ShapeDtypeStruct((TOP_K, N), jnp.int32),
            jax.ShapeDtypeStruct((TOP_K, N), F32),
            jax.ShapeDtypeStruct((TOP_K, N), jnp.int32),
            jax.ShapeDtypeStruct((N_EXPERTS, 128), jnp.int32),
        ],
        scratch_shapes=[
            pltpu.VMEM((T + CONV_HIST, D_CONV), F32),
            pltpu.VMEM((T, D_MODEL), BF16),
            pltpu.VMEM((N_EXPERTS, 128), F32),
        ],
        compiler_params=pltpu.CompilerParams(
            dimension_semantics=("arbitrary", "arbitrary"), vmem_limit_bytes=VMEM_LIMIT),
        name="mix",
    )(*mix_in)

    blk = EXPERT_BLOCK
    n_blocks = (N * TOP_K) // blk + N_EXPERTS
    n_slots = n_blocks * blk
    counts = cnt[:, 0]
    nblk_e = (counts + blk - 1) // blk
    block_end = jnp.cumsum(nblk_e)
    pad_start = (block_end - nblk_e) * blk
    dest = rank
    for e in range(N_EXPERTS):
        dest = dest + jnp.where(idx == e, pad_start[e], 0)
    dest = dest * TOKEN_LINES
    blk_ids = jnp.arange(n_blocks, dtype=jnp.int32)
    block_e = jnp.minimum(
        jnp.sum((blk_ids[:, None] >= block_end[None, :]).astype(jnp.int32), axis=1),
        N_EXPERTS - 1).astype(jnp.int32)
    total = block_end[-1:].astype(jnp.int32)

    xs = pl.pallas_call(
        _dispatch_kernel,
        grid=(N // DISPATCH_TILE,),
        in_specs=[
            pl.BlockSpec((TOP_K, DISPATCH_TILE), lambda i: (0, i), memory_space=pltpu.SMEM),
            pl.BlockSpec((DISPATCH_TILE * TOKEN_LINES, LANES), lambda i: (i, 0)),
            pl.BlockSpec(memory_space=pl.ANY),
        ],
        out_specs=pl.BlockSpec(memory_space=pl.ANY),
        out_shape=jax.ShapeDtypeStruct((n_slots * TOKEN_LINES, LANES), F32),
        scratch_shapes=[pltpu.SemaphoreType.DMA(())],
        input_output_aliases={2: 0},
        compiler_params=pltpu.CompilerParams(dimension_semantics=("arbitrary",)),
        name="dispatch",
    )(dest, h2, jnp.zeros((n_slots * TOKEN_LINES, LANES), F32))

    def blk_map(i, be, tot):
        return (jnp.minimum(i, tot[0] - 1), 0)

    def exp_map(i, be, tot):
        return (be[jnp.minimum(i, tot[0] - 1)], 0, 0)

    y = pl.pallas_call(
        _expert_kernel,
        grid_spec=pltpu.PrefetchScalarGridSpec(
            num_scalar_prefetch=2,
            grid=(n_blocks,),
            in_specs=[
                pl.BlockSpec((blk * TOKEN_LINES, LANES), blk_map),
                pl.BlockSpec((None, D, 2 * D_FF), exp_map),
                pl.BlockSpec((None, 1, 2 * D_FF), exp_map),
                pl.BlockSpec((None, D_FF, D), exp_map),
                pl.BlockSpec((None, 1, D), exp_map),
            ],
            out_specs=pl.BlockSpec((blk * TOKEN_LINES, LANES), lambda i, be, tot: (i, 0)),
        ),
        out_shape=jax.ShapeDtypeStruct((n_slots * TOKEN_LINES, LANES), F32),
        compiler_params=pltpu.CompilerParams(
            dimension_semantics=("arbitrary",), vmem_limit_bytes=VMEM_LIMIT),
        name="experts",
    )(block_e, total, xs, w_exp1[l].astype(BF16), b_exp1[l].reshape(N_EXPERTS, 1, 2 * D_FF),
      w_exp2[l].astype(BF16), b_exp2[l].reshape(N_EXPERTS, 1, D))

    TC = COMBINE_TILE
    out = pl.pallas_call(
        _combine_kernel,
        grid=(N // TC,),
        in_specs=[
            pl.BlockSpec((TOP_K, TC), lambda i: (0, i), memory_space=pltpu.SMEM),
            pl.BlockSpec((TC, D), lambda i: (i, 0)),
            pl.BlockSpec((TOP_K, TC), lambda i: (0, i)),
            _full((1, D)),
            pl.BlockSpec(memory_space=pl.ANY),
        ],
        out_specs=pl.BlockSpec((TC, D), lambda i: (i, 0)),
        out_shape=jax.ShapeDtypeStruct((N, D), F32),
        scratch_shapes=[pltpu.VMEM((TOP_K, TC * TOKEN_LINES, LANES), F32),
                        pltpu.SemaphoreType.DMA(())],
        compiler_params=pltpu.CompilerParams(
            dimension_semantics=("arbitrary",), vmem_limit_bytes=VMEM_LIMIT),
        name="combine",
    )(dest, x1.reshape(N, D), gate, row(final_g), y)
    return out.reshape(B, S, D)
```

```python
import functools

import jax
import jax.numpy as jnp
from jax import lax
from jax.experimental import pallas as pl
from jax.experimental.pallas import tpu as pltpu

D_MODEL = 1024
D_CONV = 512
D_GMLP = 512
CONV_WIDTH = 31
CONV_HIST = 32
GMLP_HEADS = 8
GMLP_HEAD_DIM = 64
GMLP_BLOCK = 128
CHUNK = 64
N_EXPERTS = 32
TOP_K = 4
D_FF = 1024
SWIGLU_ALPHA = 1.702
SWIGLU_LIMIT = 7.0
EPS = 1e-5

MIX_TILE = 512
CONV_ROWS = 32
EXPERT_BLOCK = 512
DISPATCH_TILE = 1024
COMBINE_TILE = 256
ISSUE_UNROLL = 8
VMEM_LIMIT = 56 * 1024 * 1024

F32 = jnp.float32
BF16 = jnp.bfloat16


def _rms(x, g):
    return x * lax.rsqrt(jnp.mean(x * x, axis=-1, keepdims=True) + EPS) * g


def _ln(x, g, b):
    mu = jnp.mean(x, axis=-1, keepdims=True)
    xc = x - mu
    var = jnp.mean(xc * xc, axis=-1, keepdims=True)
    return xc * lax.rsqrt(var + EPS) * g + b


def _gelu(x):
    return 0.5 * x * (1.0 + lax.erf(x * (2.0 ** -0.5)))


LANES = 128
TOKEN_LINES = D_MODEL // LANES


def _load_token_tiles(ref, rows, lead=()):
    chunks = [ref[lead + (pl.ds(c, rows, stride=TOKEN_LINES), slice(None))] for c in range(TOKEN_LINES)]
    return jnp.concatenate(chunks, axis=1)


def _store_token_tiles(ref, value, rows):
    for c in range(TOKEN_LINES):
        ref[pl.ds(c, rows, stride=TOKEN_LINES), :] = value[:, c * LANES:(c + 1) * LANES]


def _mix_kernel(x_ref, g1_ref, win_ref, bin_ref, cw_ref, cb_ref, clg_ref, clb_ref,
                slg_ref, slb_ref, sgw_ref, sgb_ref, ggc_ref, ggs_ref, wout_ref, bout_ref,
                g2_ref, wr_ref, br_ref,
                x1_ref, h2_ref, idx_ref, gate_ref, rank_ref, cnt_ref,
                ubuf, ycat, cnt_s):
    T = MIX_TILE
    b = pl.program_id(0)
    s = pl.program_id(1)

    @pl.when(s == 0)
    def _():
        ubuf[0:CONV_HIST, :] = jnp.zeros((CONV_HIST, D_CONV), F32)

    @pl.when((b == 0) & (s == 0))
    def _():
        cnt_s[...] = jnp.zeros_like(cnt_s)

    x = x_ref[...]
    h = _rms(x, g1_ref[...])
    p = jnp.dot(h.astype(BF16), win_ref[...], preferred_element_type=F32) + bin_ref[...]

    u = p[:, 0:D_CONV] * jax.nn.sigmoid(p[:, D_CONV:2 * D_CONV])
    ubuf[CONV_HIST:CONV_HIST + T, :] = u

    for c in range(T // CONV_ROWS):
        r0 = c * CONV_ROWS
        acc = jnp.broadcast_to(cb_ref[...], (CONV_ROWS, D_CONV))
        for k in range(CONV_WIDTH):
            off = r0 + CONV_HIST - (CONV_WIDTH - 1) + k
            acc = acc + cw_ref[k:k + 1, :] * ubuf[off:off + CONV_ROWS, :]
        y = _ln(acc, clg_ref[...], clb_ref[...])
        y = y * jax.nn.sigmoid(y)
        y = _rms(y, ggc_ref[...])
        ycat[r0:r0 + CONV_ROWS, 0:D_CONV] = y.astype(BF16)
    ubuf[0:CONV_HIST, :] = ubuf[T:T + CONV_HIST, :]

    uu = _gelu(p[:, 2 * D_CONV:2 * D_CONV + D_GMLP])
    vv = _ln(_gelu(p[:, 2 * D_CONV + D_GMLP:]), slg_ref[...], slb_ref[...])
    t_out = lax.broadcasted_iota(jnp.int32, (GMLP_BLOCK, 2 * GMLP_BLOCK), 0)
    s_in = lax.broadcasted_iota(jnp.int32, (GMLP_BLOCK, 2 * GMLP_BLOCK), 1) % GMLP_BLOCK
    wmask = (s_in // CHUNK) <= (t_out // CHUNK)
    lane = lax.broadcasted_iota(jnp.int32, (GMLP_BLOCK, 2 * GMLP_HEAD_DIM), 1)
    lo = lane < GMLP_HEAD_DIM
    n_pair = GMLP_HEADS // 2
    wcat = [jnp.where(wmask, sgw_ref[j], 0.0).astype(BF16) for j in range(n_pair)]
    for blk in range(T // GMLP_BLOCK):
        rows = slice(blk * GMLP_BLOCK, (blk + 1) * GMLP_BLOCK)
        outs = []
        for j in range(n_pair):
            cols = slice(j * 128, (j + 1) * 128)
            vp = vv[rows, cols]
            vstack = jnp.concatenate(
                [jnp.where(lo, vp, 0.0), jnp.where(lo, 0.0, vp)], axis=0).astype(BF16)
            mixed = jnp.dot(wcat[j], vstack, preferred_element_type=F32) + sgb_ref[:, cols]
            outs.append(uu[rows, cols] * mixed)
        ysg = jnp.concatenate(outs, axis=1)
        ycat[rows, D_CONV:] = _rms(ysg, ggs_ref[...]).astype(BF16)

    o = jnp.dot(ycat[...], wout_ref[...], preferred_element_type=F32) + bout_ref[...]
    x1 = x + o
    x1_ref[...] = x1
    h2 = _rms(x1, g2_ref[...])
    _store_token_tiles(h2_ref, h2, T)

    logits = lax.dot_general(wr_ref[...].astype(BF16), h2.astype(BF16),
                             (((1,), (1,)), ((), ())),
                             preferred_element_type=F32) + br_ref[...]
    e_iota = lax.broadcasted_iota(jnp.int32, (N_EXPERTS, T), 0)
    work = logits
    vals, idxs, sels = [], [], []
    for _k in range(TOP_K):
        m = jnp.max(work, axis=0, keepdims=True)
        am = jnp.min(jnp.where(work == m, e_iota, N_EXPERTS), axis=0, keepdims=True)
        sel = e_iota == am
        vals.append(m)
        idxs.append(am)
        sels.append(sel)
        work = jnp.where(sel, -jnp.inf, work)
    exps = [jnp.exp(v - vals[0]) for v in vals]
    denom = exps[0] + exps[1] + exps[2] + exps[3]
    gate_ref[...] = jnp.concatenate([e / denom for e in exps], axis=0)
    idx_ref[...] = jnp.concatenate(idxs, axis=0)

    member = jnp.zeros((N_EXPERTS, T), F32)
    for sel in sels:
        member = member + jnp.where(sel, 1.0, 0.0)
    r_i = lax.broadcasted_iota(jnp.int32, (T, T), 0)
    c_i = lax.broadcasted_iota(jnp.int32, (T, T), 1)
    upper = jnp.where(r_i < c_i, 1.0, 0.0).astype(BF16)
    before = jnp.dot(member.astype(BF16), upper, preferred_element_type=F32)
    before = before + cnt_s[:, 0:1]
    ranks = [jnp.sum(jnp.where(sel, before, 0.0), axis=0, keepdims=True) for sel in sels]
    rank_ref[...] = jnp.concatenate(ranks, axis=0).astype(jnp.int32)
    cnt_new = cnt_s[...] + jnp.sum(member, axis=1, keepdims=True)
    cnt_s[...] = cnt_new
    cnt_ref[...] = cnt_new.astype(jnp.int32)


def _dispatch_kernel(dest_ref, h2_ref, xs_in_ref, xs_ref, sem):
    del xs_in_ref

    def start(g, carry):
        for j in range(ISSUE_UNROLL):
            t = g * ISSUE_UNROLL + j
            src = h2_ref.at[pl.ds(pl.multiple_of(t * TOKEN_LINES, TOKEN_LINES), TOKEN_LINES)]
            for k in range(TOP_K):
                line = pl.multiple_of(dest_ref[k, t], TOKEN_LINES)
                pltpu.make_async_copy(src, xs_ref.at[pl.ds(line, TOKEN_LINES)], sem).start(priority=k % 2)
        return carry

    lax.fori_loop(0, DISPATCH_TILE // ISSUE_UNROLL, start, 0)
    for k in range(TOP_K):
        pltpu.make_async_copy(h2_ref, xs_ref.at[pl.ds(0, DISPATCH_TILE * TOKEN_LINES)], sem).wait()


def _expert_kernel(be_ref, tot_ref, xs_ref, w1_ref, b1_ref, w2_ref, b2_ref, y_ref):
    del be_ref

    @pl.when(pl.program_id(0) < tot_ref[0])
    def _():
        xb = _load_token_tiles(xs_ref, EXPERT_BLOCK).astype(BF16)
        hcat = jnp.dot(xb, w1_ref[...], preferred_element_type=F32) + b1_ref[...]
        x_glu = jnp.minimum(hcat[:, :D_FF], SWIGLU_LIMIT)
        x_lin = jnp.clip(hcat[:, D_FF:], -SWIGLU_LIMIT, SWIGLU_LIMIT)
        act = x_glu * jax.nn.sigmoid(SWIGLU_ALPHA * x_glu) * (x_lin + 1.0)
        y = jnp.dot(act.astype(BF16), w2_ref[...], preferred_element_type=F32) + b2_ref[...]
        _store_token_tiles(y_ref, y, EXPERT_BLOCK)

    @pl.when(pl.program_id(0) >= tot_ref[0])
    def _():
        y_ref[...] = jnp.zeros_like(y_ref)


def _combine_kernel(dest_ref, dnext_ref, x1_ref, gate_ref, fg_ref, y_ref, out_ref, buf, sem):
    TC = COMBINE_TILE
    i = pl.program_id(0)
    cur = i % 2
    nxt = (i + 1) % 2

    def issue(d_ref, slot):
        def start(g, carry):
            for j in range(ISSUE_UNROLL):
                t = g * ISSUE_UNROLL + j
                row = pl.multiple_of(t * TOKEN_LINES, TOKEN_LINES)
                for k in range(TOP_K):
                    line = pl.multiple_of(d_ref[k, t], TOKEN_LINES)
                    pltpu.make_async_copy(
                        y_ref.at[pl.ds(line, TOKEN_LINES)],
                        buf.at[slot, k, pl.ds(row, TOKEN_LINES)],
                        sem.at[slot]).start(priority=k % 2)
            return carry

        lax.fori_loop(0, TC // ISSUE_UNROLL, start, 0)

    @pl.when(i == 0)
    def _():
        issue(dest_ref, cur)

    @pl.when(i + 1 < pl.num_programs(0))
    def _():
        issue(dnext_ref, nxt)

    for k in range(TOP_K):
        pltpu.make_async_copy(y_ref.at[pl.ds(0, TC * TOKEN_LINES)], buf.at[cur, k],
                              sem.at[cur]).wait()

    gpad = jnp.concatenate([gate_ref[...], jnp.zeros((128 - TOP_K, TC), F32)], axis=0)
    gt = gpad.T
    acc = x1_ref[...]
    for k in range(TOP_K):
        acc = acc + gt[:, k:k + 1] * _load_token_tiles(buf, TC, lead=(cur, k))
    out_ref[...] = _rms(acc, fg_ref[...])


def _full(shape):
    n = len(shape)
    return pl.BlockSpec(shape, lambda *_: (0,) * n)


def kernel(x, norm1_g, w_in, b_in, conv_w, conv_b, conv_ln_g, conv_ln_b, sg_ln_g, sg_ln_b,
           sg_w, sg_b, grp_g_conv, grp_g_sg, w_out, b_out, norm2_g, w_router, b_router,
           w_exp1, b_exp1, w_exp2, b_exp2, final_g):
    B, S, D = x.shape
    assert D == D_MODEL and S % MIX_TILE == 0 and norm1_g.shape[0] == 1
    N = B * S
    T = MIX_TILE
    n_s = S // T
    l = 0

    row = lambda v: v.reshape(1, -1).astype(F32)
    cw = jnp.concatenate([conv_w[l], jnp.zeros((1, D_CONV), F32)], axis=0)
    sgw = sg_w[l].reshape(GMLP_HEADS // 2, 2, GMLP_BLOCK, GMLP_BLOCK)
    sgw = sgw.transpose(0, 2, 1, 3).reshape(GMLP_HEADS // 2, GMLP_BLOCK, 2 * GMLP_BLOCK)
    sgb = jnp.repeat(sg_b[l].T, GMLP_HEAD_DIM, axis=1)
    wr_t = w_router[l].T
    br = b_router[l].reshape(N_EXPERTS, 1)

    mix_in = [
        x, row(norm1_g[l]), w_in[l].astype(BF16), row(b_in[l]), cw, row(conv_b[l]),
        row(conv_ln_g[l]), row(conv_ln_b[l]), row(sg_ln_g[l]), row(sg_ln_b[l]), sgw, sgb,
        row(grp_g_conv[l]), row(grp_g_sg[l]), w_out[l].astype(BF16), row(b_out[l]),
        row(norm2_g[l]), wr_t, br,
    ]
    mix_specs = [pl.BlockSpec((None, T, D), lambda b, s: (b, s, 0))]
    mix_specs += [_full(a.shape) for a in mix_in[1:]]
    tok_spec = pl.BlockSpec((TOP_K, T), lambda b, s: (0, b * n_s + s))
    x1, h2, idx, gate, rank, cnt = pl.pallas_call(
        _mix_kernel,
        grid=(B, n_s),
        in_specs=mix_specs,
        out_specs=[
            pl.BlockSpec((None, T, D), lambda b, s: (b, s, 0)),
            pl.BlockSpec((T * TOKEN_LINES, LANES), lambda b, s: (b * n_s + s, 0)),
            tok_spec, tok_spec, tok_spec,
            pl.BlockSpec((N_EXPERTS, 128), lambda b, s: (0, 0)),
        ],
        out_shape=[
            jax.ShapeDtypeStruct((B, S, D), F32),
            jax.ShapeDtypeStruct((N * TOKEN_LINES, LANES), F32),
            jax.ShapeDtypeStruct((TOP_K, N), jnp.int32),
            jax.ShapeDtypeStruct((TOP_K, N), F32),
            jax.ShapeDtypeStruct((TOP_K, N), jnp.int32),
            jax.ShapeDtypeStruct((N_EXPERTS, 128), jnp.int32),
        ],
        scratch_shapes=[
            pltpu.VMEM((T + CONV_HIST, D_CONV), F32),
            pltpu.VMEM((T, D_MODEL), BF16),
            pltpu.VMEM((N_EXPERTS, 128), F32),
        ],
        compiler_params=pltpu.CompilerParams(
            dimension_semantics=("arbitrary", "arbitrary"), vmem_limit_bytes=VMEM_LIMIT),
        name="mix",
    )(*mix_in)

    blk = EXPERT_BLOCK
    n_blocks = (N * TOP_K) // blk + N_EXPERTS
    n_slots = n_blocks * blk
    counts = cnt[:, 0]
    nblk_e = (counts + blk - 1) // blk
    block_end = jnp.cumsum(nblk_e)
    pad_start = (block_end - nblk_e) * blk
    dest = rank
    for e in range(N_EXPERTS):
        dest = dest + jnp.where(idx == e, pad_start[e], 0)
    dest = dest * TOKEN_LINES
    blk_ids = jnp.arange(n_blocks, dtype=jnp.int32)
    block_e = jnp.minimum(
        jnp.sum((blk_ids[:, None] >= block_end[None, :]).astype(jnp.int32), axis=1),
        N_EXPERTS - 1).astype(jnp.int32)
    total = block_end[-1:].astype(jnp.int32)

    xs = pl.pallas_call(
        _dispatch_kernel,
        grid=(N // DISPATCH_TILE,),
        in_specs=[
            pl.BlockSpec((TOP_K, DISPATCH_TILE), lambda i: (0, i), memory_space=pltpu.SMEM),
            pl.BlockSpec((DISPATCH_TILE * TOKEN_LINES, LANES), lambda i: (i, 0)),
            pl.BlockSpec(memory_space=pl.ANY),
        ],
        out_specs=pl.BlockSpec(memory_space=pl.ANY),
        out_shape=jax.ShapeDtypeStruct((n_slots * TOKEN_LINES, LANES), F32),
        scratch_shapes=[pltpu.SemaphoreType.DMA(())],
        input_output_aliases={2: 0},
        compiler_params=pltpu.CompilerParams(dimension_semantics=("arbitrary",)),
        name="dispatch",
    )(dest, h2, jnp.zeros((n_slots * TOKEN_LINES, LANES), F32))

    def blk_map(i, be, tot):
        return (jnp.minimum(i, tot[0] - 1), 0)

    def exp_map(i, be, tot):
        return (be[jnp.minimum(i, tot[0] - 1)], 0, 0)

    y = pl.pallas_call(
        _expert_kernel,
        grid_spec=pltpu.PrefetchScalarGridSpec(
            num_scalar_prefetch=2,
            grid=(n_blocks,),
            in_specs=[
                pl.BlockSpec((blk * TOKEN_LINES, LANES), blk_map),
                pl.BlockSpec((None, D, 2 * D_FF), exp_map),
                pl.BlockSpec((None, 1, 2 * D_FF), exp_map),
                pl.BlockSpec((None, D_FF, D), exp_map),
                pl.BlockSpec((None, 1, D), exp_map),
            ],
            out_specs=pl.BlockSpec((blk * TOKEN_LINES, LANES), lambda i, be, tot: (i, 0)),
        ),
        out_shape=jax.ShapeDtypeStruct((n_slots * TOKEN_LINES, LANES), F32),
        compiler_params=pltpu.CompilerParams(
            dimension_semantics=("arbitrary",), vmem_limit_bytes=VMEM_LIMIT),
        name="experts",
    )(block_e, total, xs, w_exp1[l].astype(BF16), b_exp1[l].reshape(N_EXPERTS, 1, 2 * D_FF),
      w_exp2[l].astype(BF16), b_exp2[l].reshape(N_EXPERTS, 1, D))

    TC = COMBINE_TILE
    out = pl.pallas_call(
        _combine_kernel,
        grid=(N // TC,),
        in_specs=[
            pl.BlockSpec((TOP_K, TC), lambda i: (0, i), memory_space=pltpu.SMEM),
            pl.BlockSpec((TOP_K, TC), lambda i: (0, jnp.minimum(i + 1, N // TC - 1)),
                         memory_space=pltpu.SMEM),
            pl.BlockSpec((TC, D), lambda i: (i, 0)),
            pl.BlockSpec((TOP_K, TC), lambda i: (0, i)),
            _full((1, D)),
            pl.BlockSpec(memory_space=pl.ANY),
        ],
        out_specs=pl.BlockSpec((TC, D), lambda i: (i, 0)),
        out_shape=jax.ShapeDtypeStruct((N, D), F32),
        scratch_shapes=[pltpu.VMEM((2, TOP_K, TC * TOKEN_LINES, LANES), F32),
                        pltpu.SemaphoreType.DMA((2,))],
        compiler_params=pltpu.CompilerParams(
            dimension_semantics=("arbitrary",), vmem_limit_bytes=VMEM_LIMIT),
        name="combine",
    )(dest, dest, x1.reshape(N, D), gate, row(final_g), y)
    return out.reshape(B, S, D)
```

```python
import functools

import jax
import jax.numpy as jnp
from jax import lax
from jax.experimental import pallas as pl
from jax.experimental.pallas import tpu as pltpu

D_MODEL = 1024
D_CONV = 512
D_GMLP = 512
CONV_WIDTH = 31
CONV_HIST = 32
GMLP_HEADS = 8
GMLP_HEAD_DIM = 64
GMLP_BLOCK = 128
CHUNK = 64
N_EXPERTS = 32
TOP_K = 4
D_FF = 1024
SWIGLU_ALPHA = 1.702
SWIGLU_LIMIT = 7.0
EPS = 1e-5

MIX_TILE = 512
CONV_ROWS = 32
EXPERT_BLOCK = 512
DISPATCH_TILE = 1024
COMBINE_TILE = 256
ISSUE_UNROLL = 8
PAD_BITS = tuple(1 << b for b in reversed(range(EXPERT_BLOCK.bit_length() - 1)))
VMEM_LIMIT = 56 * 1024 * 1024

F32 = jnp.float32
BF16 = jnp.bfloat16


def _rms(x, g):
    return x * lax.rsqrt(jnp.mean(x * x, axis=-1, keepdims=True) + EPS) * g


def _ln(x, g, b):
    mu = jnp.mean(x, axis=-1, keepdims=True)
    xc = x - mu
    var = jnp.mean(xc * xc, axis=-1, keepdims=True)
    return xc * lax.rsqrt(var + EPS) * g + b


def _gelu(x):
    return 0.5 * x * (1.0 + lax.erf(x * (2.0 ** -0.5)))


LANES = 128
TOKEN_LINES = D_MODEL // LANES


def _load_token_tiles(ref, rows, lead=()):
    chunks = [ref[lead + (pl.ds(c, rows, stride=TOKEN_LINES), slice(None))] for c in range(TOKEN_LINES)]
    return jnp.concatenate(chunks, axis=1)


def _store_token_tiles(ref, value, rows):
    for c in range(TOKEN_LINES):
        ref[pl.ds(c, rows, stride=TOKEN_LINES), :] = value[:, c * LANES:(c + 1) * LANES]


def _mix_kernel(x_ref, g1_ref, win_ref, bin_ref, cw_ref, cb_ref, clg_ref, clb_ref,
                slg_ref, slb_ref, sgw_ref, sgb_ref, ggc_ref, ggs_ref, wout_ref, bout_ref,
                g2_ref, wr_ref, br_ref,
                x1_ref, h2_ref, idx_ref, gate_ref, rank_ref, cnt_ref,
                ubuf, ycat, cnt_s):
    T = MIX_TILE
    b = pl.program_id(0)
    s = pl.program_id(1)

    @pl.when(s == 0)
    def _():
        ubuf[0:CONV_HIST, :] = jnp.zeros((CONV_HIST, D_CONV), F32)

    @pl.when((b == 0) & (s == 0))
    def _():
        cnt_s[...] = jnp.zeros_like(cnt_s)

    x = x_ref[...]
    h = _rms(x, g1_ref[...])
    p = jnp.dot(h.astype(BF16), win_ref[...], preferred_element_type=F32) + bin_ref[...]

    u = p[:, 0:D_CONV] * jax.nn.sigmoid(p[:, D_CONV:2 * D_CONV])
    ubuf[CONV_HIST:CONV_HIST + T, :] = u

    for c in range(T // CONV_ROWS):
        r0 = c * CONV_ROWS
        acc = jnp.broadcast_to(cb_ref[...], (CONV_ROWS, D_CONV))
        for k in range(CONV_WIDTH):
            off = r0 + CONV_HIST - (CONV_WIDTH - 1) + k
            acc = acc + cw_ref[k:k + 1, :] * ubuf[off:off + CONV_ROWS, :]
        y = _ln(acc, clg_ref[...], clb_ref[...])
        y = y * jax.nn.sigmoid(y)
        y = _rms(y, ggc_ref[...])
        ycat[r0:r0 + CONV_ROWS, 0:D_CONV] = y.astype(BF16)
    ubuf[0:CONV_HIST, :] = ubuf[T:T + CONV_HIST, :]

    uu = _gelu(p[:, 2 * D_CONV:2 * D_CONV + D_GMLP])
    vv = _ln(_gelu(p[:, 2 * D_CONV + D_GMLP:]), slg_ref[...], slb_ref[...])
    t_out = lax.broadcasted_iota(jnp.int32, (GMLP_BLOCK, 2 * GMLP_BLOCK), 0)
    s_in = lax.broadcasted_iota(jnp.int32, (GMLP_BLOCK, 2 * GMLP_BLOCK), 1) % GMLP_BLOCK
    wmask = (s_in // CHUNK) <= (t_out // CHUNK)
    lane = lax.broadcasted_iota(jnp.int32, (GMLP_BLOCK, 2 * GMLP_HEAD_DIM), 1)
    lo = lane < GMLP_HEAD_DIM
    n_pair = GMLP_HEADS // 2
    wcat = [jnp.where(wmask, sgw_ref[j], 0.0).astype(BF16) for j in range(n_pair)]
    for blk in range(T // GMLP_BLOCK):
        rows = slice(blk * GMLP_BLOCK, (blk + 1) * GMLP_BLOCK)
        outs = []
        for j in range(n_pair):
            cols = slice(j * 128, (j + 1) * 128)
            vp = vv[rows, cols]
            vstack = jnp.concatenate(
                [jnp.where(lo, vp, 0.0), jnp.where(lo, 0.0, vp)], axis=0).astype(BF16)
            mixed = jnp.dot(wcat[j], vstack, preferred_element_type=F32) + sgb_ref[:, cols]
            outs.append(uu[rows, cols] * mixed)
        ysg = jnp.concatenate(outs, axis=1)
        ycat[rows, D_CONV:] = _rms(ysg, ggs_ref[...]).astype(BF16)

    o = jnp.dot(ycat[...], wout_ref[...], preferred_element_type=F32) + bout_ref[...]
    x1 = x + o
    x1_ref[...] = x1
    h2 = _rms(x1, g2_ref[...])
    _store_token_tiles(h2_ref, h2, T)

    logits = lax.dot_general(wr_ref[...].astype(BF16), h2.astype(BF16),
                             (((1,), (1,)), ((), ())),
                             preferred_element_type=F32) + br_ref[...]
    e_iota = lax.broadcasted_iota(jnp.int32, (N_EXPERTS, T), 0)
    work = logits
    vals, idxs, sels = [], [], []
    for _k in range(TOP_K):
        m = jnp.max(work, axis=0, keepdims=True)
        am = jnp.min(jnp.where(work == m, e_iota, N_EXPERTS), axis=0, keepdims=True)
        sel = e_iota == am
        vals.append(m)
        idxs.append(am)
        sels.append(sel)
        work = jnp.where(sel, -jnp.inf, work)
    exps = [jnp.exp(v - vals[0]) for v in vals]
    denom = exps[0] + exps[1] + exps[2] + exps[3]
    gate_ref[...] = jnp.concatenate([e / denom for e in exps], axis=0)
    idx_ref[...] = jnp.concatenate(idxs, axis=0)

    member = jnp.zeros((N_EXPERTS, T), F32)
    for sel in sels:
        member = member + jnp.where(sel, 1.0, 0.0)
    r_i = lax.broadcasted_iota(jnp.int32, (T, T), 0)
    c_i = lax.broadcasted_iota(jnp.int32, (T, T), 1)
    upper = jnp.where(r_i < c_i, 1.0, 0.0).astype(BF16)
    before = jnp.dot(member.astype(BF16), upper, preferred_element_type=F32)
    before = before + cnt_s[:, 0:1]
    ranks = [jnp.sum(jnp.where(sel, before, 0.0), axis=0, keepdims=True) for sel in sels]
    rank_ref[...] = jnp.concatenate(ranks, axis=0).astype(jnp.int32)
    cnt_new = cnt_s[...] + jnp.sum(member, axis=1, keepdims=True)
    cnt_s[...] = cnt_new
    cnt_ref[...] = cnt_new.astype(jnp.int32)


def _dispatch_kernel(dest_ref, padlo_ref, padn_ref, tot_ref, h2_ref, xs_ref, zbuf, sem, zsem, tsem):
    i = pl.program_id(0)
    blk_lines = EXPERT_BLOCK * TOKEN_LINES
    n_blocks = xs_ref.shape[0] // blk_lines

    def pad_pieces(e, op):
        n = padn_ref[e]
        line = padlo_ref[e]
        for bit in PAD_BITS:
            hit = (n & bit) != 0

            @pl.when(hit)
            def _(line=line, bit=bit):
                op(pltpu.make_async_copy(
                    zbuf.at[pl.ds(0, bit * TOKEN_LINES)],
                    xs_ref.at[pl.ds(pl.multiple_of(line, TOKEN_LINES), bit * TOKEN_LINES)], zsem))

            line = line + jnp.where(hit, bit * TOKEN_LINES, 0)

    def for_each_expert(op):
        def body(e, carry):
            pad_pieces(e, op)
            return carry
        lax.fori_loop(0, N_EXPERTS, body, 0)

    @pl.when(i == 0)
    def _():
        zbuf[...] = jnp.zeros_like(zbuf)
        for_each_expert(lambda c: c.start())

    tail_blk = tot_ref[0] + i

    def tail_copy():
        line = pl.multiple_of(tail_blk * blk_lines, blk_lines)
        return pltpu.make_async_copy(zbuf, xs_ref.at[pl.ds(line, blk_lines)], tsem)

    @pl.when(tail_blk < n_blocks)
    def _():
        tail_copy().start()

    def start(g, carry):
        for j in range(ISSUE_UNROLL):
            t = g * ISSUE_UNROLL + j
            src = h2_ref.at[pl.ds(pl.multiple_of(t * TOKEN_LINES, TOKEN_LINES), TOKEN_LINES)]
            for k in range(TOP_K):
                line = pl.multiple_of(dest_ref[k, t], TOKEN_LINES)
                pltpu.make_async_copy(src, xs_ref.at[pl.ds(line, TOKEN_LINES)], sem).start(priority=k % 2)
        return carry

    lax.fori_loop(0, DISPATCH_TILE // ISSUE_UNROLL, start, 0)
    for k in range(TOP_K):
        pltpu.make_async_copy(h2_ref, xs_ref.at[pl.ds(0, DISPATCH_TILE * TOKEN_LINES)], sem).wait()

    @pl.when(i == 0)
    def _():
        for_each_expert(lambda c: c.wait())

    @pl.when(tail_blk < n_blocks)
    def _():
        tail_copy().wait()


def _expert_kernel(be_ref, tot_ref, xs_ref, w1_ref, b1_ref, w2_ref, b2_ref, y_ref, w1b, w2b):
    i = pl.program_id(0)

    @pl.when(i < tot_ref[0])
    def _():
        @pl.when((i == 0) | (be_ref[i] != be_ref[jnp.maximum(i - 1, 0)]))
        def _():
            w1b[...] = w1_ref[...].astype(BF16)
            w2b[...] = w2_ref[...].astype(BF16)

        xb = _load_token_tiles(xs_ref, EXPERT_BLOCK).astype(BF16)
        hcat = jnp.dot(xb, w1b[...], preferred_element_type=F32) + b1_ref[...]
        x_glu = jnp.minimum(hcat[:, :D_FF], SWIGLU_LIMIT)
        x_lin = jnp.clip(hcat[:, D_FF:], -SWIGLU_LIMIT, SWIGLU_LIMIT)
        act = x_glu * jax.nn.sigmoid(SWIGLU_ALPHA * x_glu) * (x_lin + 1.0)
        y = jnp.dot(act.astype(BF16), w2b[...], preferred_element_type=F32) + b2_ref[...]
        _store_token_tiles(y_ref, y, EXPERT_BLOCK)

    @pl.when(i >= tot_ref[0])
    def _():
        y_ref[...] = jnp.zeros_like(y_ref)


def _combine_kernel(dest_ref, dnext_ref, x1_ref, gate_ref, fg_ref, y_ref, out_ref, buf, sem):
    TC = COMBINE_TILE
    i = pl.program_id(0)
    cur = i % 2
    nxt = (i + 1) % 2

    def issue(d_ref, slot):
        def start(g, carry):
            for j in range(ISSUE_UNROLL):
                t = g * ISSUE_UNROLL + j
                row = pl.multiple_of(t * TOKEN_LINES, TOKEN_LINES)
                for k in range(TOP_K):
                    line = pl.multiple_of(d_ref[k, t], TOKEN_LINES)
                    pltpu.make_async_copy(
                        y_ref.at[pl.ds(line, TOKEN_LINES)],
                        buf.at[slot, k, pl.ds(row, TOKEN_LINES)],
                        sem.at[slot]).start(priority=k % 2)
            return carry

        lax.fori_loop(0, TC // ISSUE_UNROLL, start, 0)

    @pl.when(i == 0)
    def _():
        issue(dest_ref, cur)

    @pl.when(i + 1 < pl.num_programs(0))
    def _():
        issue(dnext_ref, nxt)

    for k in range(TOP_K):
        pltpu.make_async_copy(y_ref.at[pl.ds(0, TC * TOKEN_LINES)], buf.at[cur, k],
                              sem.at[cur]).wait()

    gpad = jnp.concatenate([gate_ref[...], jnp.zeros((128 - TOP_K, TC), F32)], axis=0)
    gt = gpad.T
    acc = x1_ref[...]
    for k in range(TOP_K):
        acc = acc + gt[:, k:k + 1] * _load_token_tiles(buf, TC, lead=(cur, k))
    out_ref[...] = _rms(acc, fg_ref[...])


def _full(shape):
    n = len(shape)
    return pl.BlockSpec(shape, lambda *_: (0,) * n)


def kernel(x, norm1_g, w_in, b_in, conv_w, conv_b, conv_ln_g, conv_ln_b, sg_ln_g, sg_ln_b,
           sg_w, sg_b, grp_g_conv, grp_g_sg, w_out, b_out, norm2_g, w_router, b_router,
           w_exp1, b_exp1, w_exp2, b_exp2, final_g):
    B, S, D = x.shape
    assert D == D_MODEL and S % MIX_TILE == 0 and norm1_g.shape[0] == 1
    N = B * S
    T = MIX_TILE
    n_s = S // T
    l = 0

    row = lambda v: v.reshape(1, -1).astype(F32)
    cw = jnp.concatenate([conv_w[l], jnp.zeros((1, D_CONV), F32)], axis=0)
    sgw = sg_w[l].reshape(GMLP_HEADS // 2, 2, GMLP_BLOCK, GMLP_BLOCK)
    sgw = sgw.transpose(0, 2, 1, 3).reshape(GMLP_HEADS // 2, GMLP_BLOCK, 2 * GMLP_BLOCK)
    sgb = jnp.repeat(sg_b[l].T, GMLP_HEAD_DIM, axis=1)
    wr_t = w_router[l].T
    br = b_router[l].reshape(N_EXPERTS, 1)

    mix_in = [
        x, row(norm1_g[l]), w_in[l].astype(BF16), row(b_in[l]), cw, row(conv_b[l]),
        row(conv_ln_g[l]), row(conv_ln_b[l]), row(sg_ln_g[l]), row(sg_ln_b[l]), sgw, sgb,
        row(grp_g_conv[l]), row(grp_g_sg[l]), w_out[l].astype(BF16), row(b_out[l]),
        row(norm2_g[l]), wr_t, br,
    ]
    mix_specs = [pl.BlockSpec((None, T, D), lambda b, s: (b, s, 0))]
    mix_specs += [_full(a.shape) for a in mix_in[1:]]
    tok_spec = pl.BlockSpec((TOP_K, T), lambda b, s: (0, b * n_s + s))
    x1, h2, idx, gate, rank, cnt = pl.pallas_call(
        _mix_kernel,
        grid=(B, n_s),
        in_specs=mix_specs,
        out_specs=[
            pl.BlockSpec((None, T, D), lambda b, s: (b, s, 0)),
            pl.BlockSpec((T * TOKEN_LINES, LANES), lambda b, s: (b * n_s + s, 0)),
            tok_spec, tok_spec, tok_spec,
            pl.BlockSpec((N_EXPERTS, 128), lambda b, s: (0, 0)),
        ],
        out_shape=[
            jax.ShapeDtypeStruct((B, S, D), F32),
            jax.ShapeDtypeStruct((N * TOKEN_LINES, LANES), F32),
            jax.ShapeDtypeStruct((TOP_K, N), jnp.int32),
            jax.ShapeDtypeStruct((TOP_K, N), F32),
            jax.ShapeDtypeStruct((TOP_K, N), jnp.int32),
            jax.ShapeDtypeStruct((N_EXPERTS, 128), jnp.int32),
        ],
        scratch_shapes=[
            pltpu.VMEM((T + CONV_HIST, D_CONV), F32),
            pltpu.VMEM((T, D_MODEL), BF16),
            pltpu.VMEM((N_EXPERTS, 128), F32),
        ],
        compiler_params=pltpu.CompilerParams(
            dimension_semantics=("arbitrary", "arbitrary"), vmem_limit_bytes=VMEM_LIMIT),
        name="mix",
    )(*mix_in)

    blk = EXPERT_BLOCK
    n_blocks = (N * TOP_K) // blk + N_EXPERTS
    n_slots = n_blocks * blk
    counts = cnt[:, 0]
    nblk_e = (counts + blk - 1) // blk
    block_end = jnp.cumsum(nblk_e)
    pad_start = (block_end - nblk_e) * blk
    dest = rank
    for e in range(N_EXPERTS):
        dest = dest + jnp.where(idx == e, pad_start[e], 0)
    dest = dest * TOKEN_LINES
    blk_ids = jnp.arange(n_blocks, dtype=jnp.int32)
    block_e = jnp.minimum(
        jnp.sum((blk_ids[:, None] >= block_end[None, :]).astype(jnp.int32), axis=1),
        N_EXPERTS - 1).astype(jnp.int32)
    total = block_end[-1:].astype(jnp.int32)
    pad_lo = ((pad_start + counts) * TOKEN_LINES).astype(jnp.int32)
    pad_n = (nblk_e * blk - counts).astype(jnp.int32)
    assert N // DISPATCH_TILE >= N_EXPERTS

    xs = pl.pallas_call(
        _dispatch_kernel,
        grid=(N // DISPATCH_TILE,),
        in_specs=[
            pl.BlockSpec((TOP_K, DISPATCH_TILE), lambda i: (0, i), memory_space=pltpu.SMEM),
            pl.BlockSpec(memory_space=pltpu.SMEM),
            pl.BlockSpec(memory_space=pltpu.SMEM),
            pl.BlockSpec(memory_space=pltpu.SMEM),
            pl.BlockSpec((DISPATCH_TILE * TOKEN_LINES, LANES), lambda i: (i, 0)),
        ],
        out_specs=pl.BlockSpec(memory_space=pl.ANY),
        out_shape=jax.ShapeDtypeStruct((n_slots * TOKEN_LINES, LANES), F32),
        scratch_shapes=[pltpu.VMEM((blk * TOKEN_LINES, LANES), F32),
                        pltpu.SemaphoreType.DMA(()), pltpu.SemaphoreType.DMA(()),
                        pltpu.SemaphoreType.DMA(())],
        compiler_params=pltpu.CompilerParams(
            dimension_semantics=("arbitrary",), vmem_limit_bytes=VMEM_LIMIT),
        name="dispatch",
    )(dest, pad_lo, pad_n, total, h2)

    def blk_map(i, be, tot):
        return (jnp.minimum(i, tot[0] - 1), 0)

    def exp_map(i, be, tot):
        return (be[jnp.minimum(i, tot[0] - 1)], 0, 0)

    y = pl.pallas_call(
        _expert_kernel,
        grid_spec=pltpu.PrefetchScalarGridSpec(
            num_scalar_prefetch=2,
            grid=(n_blocks,),
            in_specs=[
                pl.BlockSpec((blk * TOKEN_LINES, LANES), blk_map),
                pl.BlockSpec((None, D, 2 * D_FF), exp_map),
                pl.BlockSpec((None, 1, 2 * D_FF), exp_map),
                pl.BlockSpec((None, D_FF, D), exp_map),
                pl.BlockSpec((None, 1, D), exp_map),
            ],
            out_specs=pl.BlockSpec((blk * TOKEN_LINES, LANES), lambda i, be, tot: (i, 0)),
            scratch_shapes=[pltpu.VMEM((D, 2 * D_FF), BF16), pltpu.VMEM((D_FF, D), BF16)],
        ),
        out_shape=jax.ShapeDtypeStruct((n_slots * TOKEN_LINES, LANES), F32),
        compiler_params=pltpu.CompilerParams(
            dimension_semantics=("arbitrary",), vmem_limit_bytes=VMEM_LIMIT),
        name="experts",
    )(block_e, total, xs, w_exp1[l], b_exp1[l].reshape(N_EXPERTS, 1, 2 * D_FF),
      w_exp2[l], b_exp2[l].reshape(N_EXPERTS, 1, D))

    TC = COMBINE_TILE
    out = pl.pallas_call(
        _combine_kernel,
        grid=(N // TC,),
        in_specs=[
            pl.BlockSpec((TOP_K, TC), lambda i: (0, i), memory_space=pltpu.SMEM),
            pl.BlockSpec((TOP_K, TC), lambda i: (0, jnp.minimum(i + 1, N // TC - 1)),
                         memory_space=pltpu.SMEM),
            pl.BlockSpec((TC, D), lambda i: (i, 0)),
            pl.BlockSpec((TOP_K, TC), lambda i: (0, i)),
            _full((1, D)),
            pl.BlockSpec(memory_space=pl.ANY),
        ],
        out_specs=pl.BlockSpec((TC, D), lambda i: (i, 0)),
        out_shape=jax.ShapeDtypeStruct((N, D), F32),
        scratch_shapes=[pltpu.VMEM((2, TOP_K, TC * TOKEN_LINES, LANES), F32),
                        pltpu.SemaphoreType.DMA((2,))],
        compiler_params=pltpu.CompilerParams(
            dimension_semantics=("arbitrary",), vmem_limit_bytes=VMEM_LIMIT),
        name="combine",
    )(dest, dest, x1.reshape(N, D), gate, row(final_g), y)
    return out.reshape(B, S, D)
```

```python
import functools

import jax
import jax.numpy as jnp
from jax import lax
from jax.experimental import pallas as pl
from jax.experimental.pallas import tpu as pltpu

D_MODEL = 1024
D_CONV = 512
D_GMLP = 512
CONV_WIDTH = 31
CONV_HIST = 32
GMLP_HEADS = 8
GMLP_HEAD_DIM = 64
GMLP_BLOCK = 128
CHUNK = 64
N_EXPERTS = 32
TOP_K = 4
D_FF = 1024
SWIGLU_ALPHA = 1.702
SWIGLU_LIMIT = 7.0
EPS = 1e-5

MIX_TILE = 512
CONV_ROWS = 32
SUBLANES = 8
SHIFT_ROWS = 136
EXPERT_BLOCK = 512
DISPATCH_TILE = 1024
COMBINE_TILE = 256
ISSUE_UNROLL = 8
PAD_BITS = tuple(1 << b for b in reversed(range(EXPERT_BLOCK.bit_length() - 1)))
VMEM_LIMIT = 56 * 1024 * 1024

F32 = jnp.float32
BF16 = jnp.bfloat16


def _rms(x, g):
    return x * lax.rsqrt(jnp.mean(x * x, axis=-1, keepdims=True) + EPS) * g


def _ln(x, g, b):
    mu = jnp.mean(x, axis=-1, keepdims=True)
    xc = x - mu
    var = jnp.mean(xc * xc, axis=-1, keepdims=True)
    return xc * lax.rsqrt(var + EPS) * g + b


def _gelu(x):
    return 0.5 * x * (1.0 + lax.erf(x * (2.0 ** -0.5)))


LANES = 128
TOKEN_LINES = D_MODEL // LANES


def _load_token_tiles(ref, rows, lead=()):
    chunks = [ref[lead + (pl.ds(c, rows, stride=TOKEN_LINES), slice(None))] for c in range(TOKEN_LINES)]
    return jnp.concatenate(chunks, axis=1)


def _store_token_tiles(ref, value, rows):
    for c in range(TOKEN_LINES):
        ref[pl.ds(c, rows, stride=TOKEN_LINES), :] = value[:, c * LANES:(c + 1) * LANES]


def _mix_kernel(x_ref, g1_ref, win_ref, bin_ref, cw_ref, cb_ref, clg_ref, clb_ref,
                slg_ref, slb_ref, sgw_ref, sgb_ref, ggc_ref, ggs_ref, wout_ref, bout_ref,
                g2_ref, wr_ref, br_ref,
                x1_ref, h2_ref, idx_ref, gate_ref, rank_ref, cnt_ref,
                ubuf, ushift, ycat, cnt_s):
    T = MIX_TILE
    b = pl.program_id(0)
    s = pl.program_id(1)

    @pl.when(s == 0)
    def _():
        ubuf[0:CONV_HIST, :] = jnp.zeros((CONV_HIST, D_CONV), F32)

    @pl.when((b == 0) & (s == 0))
    def _():
        cnt_s[...] = jnp.zeros_like(cnt_s)

    x = x_ref[...]
    h = _rms(x, g1_ref[...])
    p = jnp.dot(h.astype(BF16), win_ref[...], preferred_element_type=F32) + bin_ref[...]

    u = p[:, 0:D_CONV] * jax.nn.sigmoid(p[:, D_CONV:2 * D_CONV])
    ubuf[CONV_HIST:CONV_HIST + T, :] = u

    span = T + CONV_HIST - SUBLANES
    for r in range(1, SUBLANES):
        for j0 in range(0, span, SHIFT_ROWS):
            n = min(SHIFT_ROWS, span - j0)
            ushift[r - 1, j0:j0 + n, :] = ubuf[j0 + r:j0 + r + n, :]

    for c in range(T // CONV_ROWS):
        r0 = c * CONV_ROWS
        acc = jnp.broadcast_to(cb_ref[...], (CONV_ROWS, D_CONV))
        for k in range(CONV_WIDTH):
            off = CONV_HIST - (CONV_WIDTH - 1) + k
            q, r = divmod(off, SUBLANES)
            a0 = r0 + q * SUBLANES
            if r == 0:
                tap = ubuf[a0:a0 + CONV_ROWS, :]
            else:
                tap = ushift[r - 1, a0:a0 + CONV_ROWS, :]
            acc = acc + cw_ref[k:k + 1, :] * tap
        y = _ln(acc, clg_ref[...], clb_ref[...])
        y = y * jax.nn.sigmoid(y)
        y = _rms(y, ggc_ref[...])
        ycat[r0:r0 + CONV_ROWS, 0:D_CONV] = y.astype(BF16)
    ubuf[0:CONV_HIST, :] = ubuf[T:T + CONV_HIST, :]

    uu = _gelu(p[:, 2 * D_CONV:2 * D_CONV + D_GMLP])
    vv = _ln(_gelu(p[:, 2 * D_CONV + D_GMLP:]), slg_ref[...], slb_ref[...])
    t_out = lax.broadcasted_iota(jnp.int32, (GMLP_BLOCK, 2 * GMLP_BLOCK), 0)
    s_in = lax.broadcasted_iota(jnp.int32, (GMLP_BLOCK, 2 * GMLP_BLOCK), 1) % GMLP_BLOCK
    wmask = (s_in // CHUNK) <= (t_out // CHUNK)
    lane = lax.broadcasted_iota(jnp.int32, (GMLP_BLOCK, 2 * GMLP_HEAD_DIM), 1)
    lo = lane < GMLP_HEAD_DIM
    n_pair = GMLP_HEADS // 2
    wcat = [jnp.where(wmask, sgw_ref[j], 0.0).astype(BF16) for j in range(n_pair)]
    for blk in range(T // GMLP_BLOCK):
        rows = slice(blk * GMLP_BLOCK, (blk + 1) * GMLP_BLOCK)
        outs = []
        for j in range(n_pair):
            cols = slice(j * 128, (j + 1) * 128)
            vp = vv[rows, cols]
            vstack = jnp.concatenate(
                [jnp.where(lo, vp, 0.0), jnp.where(lo, 0.0, vp)], axis=0).astype(BF16)
            mixed = jnp.dot(wcat[j], vstack, preferred_element_type=F32) + sgb_ref[:, cols]
            outs.append(uu[rows, cols] * mixed)
        ysg = jnp.concatenate(outs, axis=1)
        ycat[rows, D_CONV:] = _rms(ysg, ggs_ref[...]).astype(BF16)

    o = jnp.dot(ycat[...], wout_ref[...], preferred_element_type=F32) + bout_ref[...]
    x1 = x + o
    x1_ref[...] = x1
    h2 = _rms(x1, g2_ref[...])
    _store_token_tiles(h2_ref, h2, T)

    logits = lax.dot_general(wr_ref[...].astype(BF16), h2.astype(BF16),
                             (((1,), (1,)), ((), ())),
                             preferred_element_type=F32) + br_ref[...]
    e_iota = lax.broadcasted_iota(jnp.int32, (N_EXPERTS, T), 0)
    work = logits
    vals, idxs, sels = [], [], []
    for _k in range(TOP_K):
        m = jnp.max(work, axis=0, keepdims=True)
        am = jnp.min(jnp.where(work == m, e_iota, N_EXPERTS), axis=0, keepdims=True)
        sel = e_iota == am
        vals.append(m)
        idxs.append(am)
        sels.append(sel)
        work = jnp.where(sel, -jnp.inf, work)
    exps = [jnp.exp(v - vals[0]) for v in vals]
    denom = exps[0] + exps[1] + exps[2] + exps[3]
    gate_ref[...] = jnp.concatenate([e / denom for e in exps], axis=0)
    idx_ref[...] = jnp.concatenate(idxs, axis=0)

    member = jnp.zeros((N_EXPERTS, T), F32)
    for sel in sels:
        member = member + jnp.where(sel, 1.0, 0.0)
    r_i = lax.broadcasted_iota(jnp.int32, (T, T), 0)
    c_i = lax.broadcasted_iota(jnp.int32, (T, T), 1)
    upper = jnp.where(r_i < c_i, 1.0, 0.0).astype(BF16)
    before = jnp.dot(member.astype(BF16), upper, preferred_element_type=F32)
    before = before + cnt_s[:, 0:1]
    ranks = [jnp.sum(jnp.where(sel, before, 0.0), axis=0, keepdims=True) for sel in sels]
    rank_ref[...] = jnp.concatenate(ranks, axis=0).astype(jnp.int32)
    cnt_new = cnt_s[...] + jnp.sum(member, axis=1, keepdims=True)
    cnt_s[...] = cnt_new
    cnt_ref[...] = cnt_new.astype(jnp.int32)


def _dispatch_kernel(dest_ref, padlo_ref, padn_ref, tot_ref, h2_ref, xs_ref, zbuf, sem, zsem, tsem):
    i = pl.program_id(0)
    blk_lines = EXPERT_BLOCK * TOKEN_LINES
    n_blocks = xs_ref.shape[0] // blk_lines

    def pad_pieces(e, op):
        n = padn_ref[e]
        line = padlo_ref[e]
        for bit in PAD_BITS:
            hit = (n & bit) != 0

            @pl.when(hit)
            def _(line=line, bit=bit):
                op(pltpu.make_async_copy(
                    zbuf.at[pl.ds(0, bit * TOKEN_LINES)],
                    xs_ref.at[pl.ds(pl.multiple_of(line, TOKEN_LINES), bit * TOKEN_LINES)], zsem))

            line = line + jnp.where(hit, bit * TOKEN_LINES, 0)

    def for_each_expert(op):
        def body(e, carry):
            pad_pieces(e, op)
            return carry
        lax.fori_loop(0, N_EXPERTS, body, 0)

    @pl.when(i == 0)
    def _():
        zbuf[...] = jnp.zeros_like(zbuf)
        for_each_expert(lambda c: c.start())

    tail_blk = tot_ref[0] + i

    def tail_copy():
        line = pl.multiple_of(tail_blk * blk_lines, blk_lines)
        return pltpu.make_async_copy(zbuf, xs_ref.at[pl.ds(line, blk_lines)], tsem)

    @pl.when(tail_blk < n_blocks)
    def _():
        tail_copy().start()

    def start(g, carry):
        for j in range(ISSUE_UNROLL):
            t = g * ISSUE_UNROLL + j
            src = h2_ref.at[pl.ds(pl.multiple_of(t * TOKEN_LINES, TOKEN_LINES), TOKEN_LINES)]
            for k in range(TOP_K):
                line = pl.multiple_of(dest_ref[k, t], TOKEN_LINES)
                pltpu.make_async_copy(src, xs_ref.at[pl.ds(line, TOKEN_LINES)], sem).start(priority=k % 2)
        return carry

    lax.fori_loop(0, DISPATCH_TILE // ISSUE_UNROLL, start, 0)
    for k in range(TOP_K):
        pltpu.make_async_copy(h2_ref, xs_ref.at[pl.ds(0, DISPATCH_TILE * TOKEN_LINES)], sem).wait()

    @pl.when(i == 0)
    def _():
        for_each_expert(lambda c: c.wait())

    @pl.when(tail_blk < n_blocks)
    def _():
        tail_copy().wait()


def _expert_kernel(be_ref, tot_ref, xs_ref, w1_ref, b1_ref, w2_ref, b2_ref, y_ref, w1b, w2b):
    i = pl.program_id(0)

    @pl.when(i < tot_ref[0])
    def _():
        @pl.when((i == 0) | (be_ref[i] != be_ref[jnp.maximum(i - 1, 0)]))
        def _():
            w1b[...] = w1_ref[...].astype(BF16)
            w2b[...] = w2_ref[...].astype(BF16)

        xb = _load_token_tiles(xs_ref, EXPERT_BLOCK).astype(BF16)
        hcat = jnp.dot(xb, w1b[...], preferred_element_type=F32) + b1_ref[...]
        x_glu = jnp.minimum(hcat[:, :D_FF], SWIGLU_LIMIT)
        x_lin = jnp.clip(hcat[:, D_FF:], -SWIGLU_LIMIT, SWIGLU_LIMIT)
        act = x_glu * jax.nn.sigmoid(SWIGLU_ALPHA * x_glu) * (x_lin + 1.0)
        y = jnp.dot(act.astype(BF16), w2b[...], preferred_element_type=F32) + b2_ref[...]
        _store_token_tiles(y_ref, y, EXPERT_BLOCK)

    @pl.when(i >= tot_ref[0])
    def _():
        y_ref[...] = jnp.zeros_like(y_ref)


def _combine_kernel(dest_ref, dnext_ref, x1_ref, gate_ref, fg_ref, y_ref, out_ref, buf, sem):
    TC = COMBINE_TILE
    i = pl.program_id(0)
    cur = i % 2
    nxt = (i + 1) % 2

    def issue(d_ref, slot):
        def start(g, carry):
            for j in range(ISSUE_UNROLL):
                t = g * ISSUE_UNROLL + j
                row = pl.multiple_of(t * TOKEN_LINES, TOKEN_LINES)
                for k in range(TOP_K):
                    line = pl.multiple_of(d_ref[k, t], TOKEN_LINES)
                    pltpu.make_async_copy(
                        y_ref.at[pl.ds(line, TOKEN_LINES)],
                        buf.at[slot, k, pl.ds(row, TOKEN_LINES)],
                        sem.at[slot]).start(priority=k % 2)
            return carry

        lax.fori_loop(0, TC // ISSUE_UNROLL, start, 0)

    @pl.when(i == 0)
    def _():
        issue(dest_ref, cur)

    @pl.when(i + 1 < pl.num_programs(0))
    def _():
        issue(dnext_ref, nxt)

    for k in range(TOP_K):
        pltpu.make_async_copy(y_ref.at[pl.ds(0, TC * TOKEN_LINES)], buf.at[cur, k],
                              sem.at[cur]).wait()

    gpad = jnp.concatenate([gate_ref[...], jnp.zeros((128 - TOP_K, TC), F32)], axis=0)
    gt = gpad.T
    acc = x1_ref[...]
    for k in range(TOP_K):
        acc = acc + gt[:, k:k + 1] * _load_token_tiles(buf, TC, lead=(cur, k))
    out_ref[...] = _rms(acc, fg_ref[...])


def _full(shape):
    n = len(shape)
    return pl.BlockSpec(shape, lambda *_: (0,) * n)


def kernel(x, norm1_g, w_in, b_in, conv_w, conv_b, conv_ln_g, conv_ln_b, sg_ln_g, sg_ln_b,
           sg_w, sg_b, grp_g_conv, grp_g_sg, w_out, b_out, norm2_g, w_router, b_router,
           w_exp1, b_exp1, w_exp2, b_exp2, final_g):
    B, S, D = x.shape
    assert D == D_MODEL and S % MIX_TILE == 0 and norm1_g.shape[0] == 1
    N = B * S
    T = MIX_TILE
    n_s = S // T
    l = 0

    row = lambda v: v.reshape(1, -1).astype(F32)
    cw = jnp.concatenate([conv_w[l], jnp.zeros((1, D_CONV), F32)], axis=0)
    sgw = sg_w[l].reshape(GMLP_HEADS // 2, 2, GMLP_BLOCK, GMLP_BLOCK)
    sgw = sgw.transpose(0, 2, 1, 3).reshape(GMLP_HEADS // 2, GMLP_BLOCK, 2 * GMLP_BLOCK)
    sgb = jnp.repeat(sg_b[l].T, GMLP_HEAD_DIM, axis=1)
    wr_t = w_router[l].T
    br = b_router[l].reshape(N_EXPERTS, 1)

    mix_in = [
        x, row(norm1_g[l]), w_in[l].astype(BF16), row(b_in[l]), cw, row(conv_b[l]),
        row(conv_ln_g[l]), row(conv_ln_b[l]), row(sg_ln_g[l]), row(sg_ln_b[l]), sgw, sgb,
        row(grp_g_conv[l]), row(grp_g_sg[l]), w_out[l].astype(BF16), row(b_out[l]),
        row(norm2_g[l]), wr_t, br,
    ]
    mix_specs = [pl.BlockSpec((None, T, D), lambda b, s: (b, s, 0))]
    mix_specs += [_full(a.shape) for a in mix_in[1:]]
    tok_spec = pl.BlockSpec((TOP_K, T), lambda b, s: (0, b * n_s + s))
    x1, h2, idx, gate, rank, cnt = pl.pallas_call(
        _mix_kernel,
        grid=(B, n_s),
        in_specs=mix_specs,
        out_specs=[
            pl.BlockSpec((None, T, D), lambda b, s: (b, s, 0)),
            pl.BlockSpec((T * TOKEN_LINES, LANES), lambda b, s: (b * n_s + s, 0)),
            tok_spec, tok_spec, tok_spec,
            pl.BlockSpec((N_EXPERTS, 128), lambda b, s: (0, 0)),
        ],
        out_shape=[
            jax.ShapeDtypeStruct((B, S, D), F32),
            jax.ShapeDtypeStruct((N * TOKEN_LINES, LANES), F32),
            jax.ShapeDtypeStruct((TOP_K, N), jnp.int32),
            jax.ShapeDtypeStruct((TOP_K, N), F32),
            jax.ShapeDtypeStruct((TOP_K, N), jnp.int32),
            jax.ShapeDtypeStruct((N_EXPERTS, 128), jnp.int32),
        ],
        scratch_shapes=[
            pltpu.VMEM((T + CONV_HIST, D_CONV), F32),
            pltpu.VMEM((SUBLANES - 1, T + CONV_HIST, D_CONV), F32),
            pltpu.VMEM((T, D_MODEL), BF16),
            pltpu.VMEM((N_EXPERTS, 128), F32),
        ],
        compiler_params=pltpu.CompilerParams(
            dimension_semantics=("arbitrary", "arbitrary"), vmem_limit_bytes=VMEM_LIMIT),
        name="mix",
    )(*mix_in)

    blk = EXPERT_BLOCK
    n_blocks = (N * TOP_K) // blk + N_EXPERTS
    n_slots = n_blocks * blk
    counts = cnt[:, 0]
    nblk_e = (counts + blk - 1) // blk
    block_end = jnp.cumsum(nblk_e)
    pad_start = (block_end - nblk_e) * blk
    dest = rank
    for e in range(N_EXPERTS):
        dest = dest + jnp.where(idx == e, pad_start[e], 0)
    dest = dest * TOKEN_LINES
    blk_ids = jnp.arange(n_blocks, dtype=jnp.int32)
    block_e = jnp.minimum(
        jnp.sum((blk_ids[:, None] >= block_end[None, :]).astype(jnp.int32), axis=1),
        N_EXPERTS - 1).astype(jnp.int32)
    total = block_end[-1:].astype(jnp.int32)
    pad_lo = ((pad_start + counts) * TOKEN_LINES).astype(jnp.int32)
    pad_n = (nblk_e * blk - counts).astype(jnp.int32)
    assert N // DISPATCH_TILE >= N_EXPERTS

    xs = pl.pallas_call(
        _dispatch_kernel,
        grid=(N // DISPATCH_TILE,),
        in_specs=[
            pl.BlockSpec((TOP_K, DISPATCH_TILE), lambda i: (0, i), memory_space=pltpu.SMEM),
            pl.BlockSpec(memory_space=pltpu.SMEM),
            pl.BlockSpec(memory_space=pltpu.SMEM),
            pl.BlockSpec(memory_space=pltpu.SMEM),
            pl.BlockSpec((DISPATCH_TILE * TOKEN_LINES, LANES), lambda i: (i, 0)),
        ],
        out_specs=pl.BlockSpec(memory_space=pl.ANY),
        out_shape=jax.ShapeDtypeStruct((n_slots * TOKEN_LINES, LANES), F32),
        scratch_shapes=[pltpu.VMEM((blk * TOKEN_LINES, LANES), F32),
                        pltpu.SemaphoreType.DMA(()), pltpu.SemaphoreType.DMA(()),
                        pltpu.SemaphoreType.DMA(())],
        compiler_params=pltpu.CompilerParams(
            dimension_semantics=("arbitrary",), vmem_limit_bytes=VMEM_LIMIT),
        name="dispatch",
    )(dest, pad_lo, pad_n, total, h2)

    def blk_map(i, be, tot):
        return (jnp.minimum(i, tot[0] - 1), 0)

    def exp_map(i, be, tot):
        return (be[jnp.minimum(i, tot[0] - 1)], 0, 0)

    y = pl.pallas_call(
        _expert_kernel,
        grid_spec=pltpu.PrefetchScalarGridSpec(
            num_scalar_prefetch=2,
            grid=(n_blocks,),
            in_specs=[
                pl.BlockSpec((blk * TOKEN_LINES, LANES), blk_map),
                pl.BlockSpec((None, D, 2 * D_FF), exp_map),
                pl.BlockSpec((None, 1, 2 * D_FF), exp_map),
                pl.BlockSpec((None, D_FF, D), exp_map),
                pl.BlockSpec((None, 1, D), exp_map),
            ],
            out_specs=pl.BlockSpec((blk * TOKEN_LINES, LANES), lambda i, be, tot: (i, 0)),
            scratch_shapes=[pltpu.VMEM((D, 2 * D_FF), BF16), pltpu.VMEM((D_FF, D), BF16)],
        ),
        out_shape=jax.ShapeDtypeStruct((n_slots * TOKEN_LINES, LANES), F32),
        compiler_params=pltpu.CompilerParams(
            dimension_semantics=("arbitrary",), vmem_limit_bytes=VMEM_LIMIT),
        name="experts",
    )(block_e, total, xs, w_exp1[l], b_exp1[l].reshape(N_EXPERTS, 1, 2 * D_FF),
      w_exp2[l], b_exp2[l].reshape(N_EXPERTS, 1, D))

    TC = COMBINE_TILE
    out = pl.pallas_call(
        _combine_kernel,
        grid=(N // TC,),
        in_specs=[
            pl.BlockSpec((TOP_K, TC), lambda i: (0, i), memory_space=pltpu.SMEM),
            pl.BlockSpec((TOP_K, TC), lambda i: (0, jnp.minimum(i + 1, N // TC - 1)),
                         memory_space=pltpu.SMEM),
            pl.BlockSpec((TC, D), lambda i: (i, 0)),
            pl.BlockSpec((TOP_K, TC), lambda i: (0, i)),
            _full((1, D)),
            pl.BlockSpec(memory_space=pl.ANY),
        ],
        out_specs=pl.BlockSpec((TC, D), lambda i: (i, 0)),
        out_shape=jax.ShapeDtypeStruct((N, D), F32),
        scratch_shapes=[pltpu.VMEM((2, TOP_K, TC * TOKEN_LINES, LANES), F32),
                        pltpu.SemaphoreType.DMA((2,))],
        compiler_params=pltpu.CompilerParams(
            dimension_semantics=("arbitrary",), vmem_limit_bytes=VMEM_LIMIT),
        name="combine",
    )(dest, dest, x1.reshape(N, D), gate, row(final_g), y)
    return out.reshape(B, S, D)
```

```python
import functools

import jax
import jax.numpy as jnp
from jax import lax
from jax.experimental import pallas as pl
from jax.experimental.pallas import tpu as pltpu

D_MODEL = 1024
D_CONV = 512
D_GMLP = 512
CONV_WIDTH = 31
CONV_HIST = 32
GMLP_HEADS = 8
GMLP_HEAD_DIM = 64
GMLP_BLOCK = 128
CHUNK = 64
N_EXPERTS = 32
TOP_K = 4
D_FF = 1024
SWIGLU_ALPHA = 1.702
SWIGLU_LIMIT = 7.0
EPS = 1e-5

MIX_TILE = 512
CONV_ROWS = 32
SUBLANES = 8
SHIFT_ROWS = 136
EXPERT_BLOCK = 512
DISPATCH_TILE = 1024
COMBINE_TILE = 256
ISSUE_UNROLL = 8
PAD_BITS = tuple(1 << b for b in reversed(range(EXPERT_BLOCK.bit_length() - 1)))
VMEM_LIMIT = 56 * 1024 * 1024

F32 = jnp.float32
BF16 = jnp.bfloat16


def _rms(x, g):
    return x * lax.rsqrt(jnp.mean(x * x, axis=-1, keepdims=True) + EPS) * g


def _ln(x, g, b):
    mu = jnp.mean(x, axis=-1, keepdims=True)
    xc = x - mu
    var = jnp.mean(xc * xc, axis=-1, keepdims=True)
    return xc * lax.rsqrt(var + EPS) * g + b


def _gelu(x):
    return 0.5 * x * (1.0 + lax.erf(x * (2.0 ** -0.5)))


LANES = 128
TOKEN_LINES = D_MODEL // LANES


GATHER_PITCH = TOKEN_LINES + 1


def _load_token_tiles(ref, rows, lead=(), pitch=TOKEN_LINES):
    chunks = [ref[lead + (pl.ds(c, rows, stride=pitch), slice(None))] for c in range(TOKEN_LINES)]
    return jnp.concatenate(chunks, axis=1)


def _store_token_tiles(ref, value, rows):
    for c in range(TOKEN_LINES):
        ref[pl.ds(c, rows, stride=TOKEN_LINES), :] = value[:, c * LANES:(c + 1) * LANES]


def _mix_kernel(x_ref, g1_ref, win_ref, bin_ref, cw_ref, cb_ref, clg_ref, clb_ref,
                slg_ref, slb_ref, sgw_ref, sgb_ref, ggc_ref, ggs_ref, wout_ref, bout_ref,
                g2_ref, wr_ref, br_ref,
                x1_ref, h2_ref, idx_ref, gate_ref, rank_ref, cnt_ref,
                ubuf, ushift, ycat, cnt_s):
    T = MIX_TILE
    b = pl.program_id(0)
    s = pl.program_id(1)

    @pl.when(s == 0)
    def _():
        ubuf[0:CONV_HIST, :] = jnp.zeros((CONV_HIST, D_CONV), F32)

    @pl.when((b == 0) & (s == 0))
    def _():
        cnt_s[...] = jnp.zeros_like(cnt_s)

    x = x_ref[...]
    h = _rms(x, g1_ref[...])
    p = jnp.dot(h.astype(BF16), win_ref[...], preferred_element_type=F32) + bin_ref[...]

    u = p[:, 0:D_CONV] * jax.nn.sigmoid(p[:, D_CONV:2 * D_CONV])
    ubuf[CONV_HIST:CONV_HIST + T, :] = u

    span = T + CONV_HIST - SUBLANES
    for r in range(1, SUBLANES):
        for j0 in range(0, span, SHIFT_ROWS):
            n = min(SHIFT_ROWS, span - j0)
            ushift[r - 1, j0:j0 + n, :] = ubuf[j0 + r:j0 + r + n, :]

    for c in range(T // CONV_ROWS):
        r0 = c * CONV_ROWS
        acc = jnp.broadcast_to(cb_ref[...], (CONV_ROWS, D_CONV))
        for k in range(CONV_WIDTH):
            off = CONV_HIST - (CONV_WIDTH - 1) + k
            q, r = divmod(off, SUBLANES)
            a0 = r0 + q * SUBLANES
            if r == 0:
                tap = ubuf[a0:a0 + CONV_ROWS, :]
            else:
                tap = ushift[r - 1, a0:a0 + CONV_ROWS, :]
            acc = acc + cw_ref[k:k + 1, :] * tap
        y = _ln(acc, clg_ref[...], clb_ref[...])
        y = y * jax.nn.sigmoid(y)
        y = _rms(y, ggc_ref[...])
        ycat[r0:r0 + CONV_ROWS, 0:D_CONV] = y.astype(BF16)
    ubuf[0:CONV_HIST, :] = ubuf[T:T + CONV_HIST, :]

    uu = _gelu(p[:, 2 * D_CONV:2 * D_CONV + D_GMLP])
    vv = _ln(_gelu(p[:, 2 * D_CONV + D_GMLP:]), slg_ref[...], slb_ref[...])
    t_out = lax.broadcasted_iota(jnp.int32, (GMLP_BLOCK, 2 * GMLP_BLOCK), 0)
    s_in = lax.broadcasted_iota(jnp.int32, (GMLP_BLOCK, 2 * GMLP_BLOCK), 1) % GMLP_BLOCK
    wmask = (s_in // CHUNK) <= (t_out // CHUNK)
    lane = lax.broadcasted_iota(jnp.int32, (GMLP_BLOCK, 2 * GMLP_HEAD_DIM), 1)
    lo = lane < GMLP_HEAD_DIM
    n_pair = GMLP_HEADS // 2
    wcat = [jnp.where(wmask, sgw_ref[j], 0.0).astype(BF16) for j in range(n_pair)]
    for blk in range(T // GMLP_BLOCK):
        rows = slice(blk * GMLP_BLOCK, (blk + 1) * GMLP_BLOCK)
        outs = []
        for j in range(n_pair):
            cols = slice(j * 128, (j + 1) * 128)
            vp = vv[rows, cols]
            vstack = jnp.concatenate(
                [jnp.where(lo, vp, 0.0), jnp.where(lo, 0.0, vp)], axis=0).astype(BF16)
            mixed = jnp.dot(wcat[j], vstack, preferred_element_type=F32) + sgb_ref[:, cols]
            outs.append(uu[rows, cols] * mixed)
        ysg = jnp.concatenate(outs, axis=1)
        ycat[rows, D_CONV:] = _rms(ysg, ggs_ref[...]).astype(BF16)

    o = jnp.dot(ycat[...], wout_ref[...], preferred_element_type=F32) + bout_ref[...]
    x1 = x + o
    x1_ref[...] = x1
    h2 = _rms(x1, g2_ref[...])
    _store_token_tiles(h2_ref, h2, T)

    logits = lax.dot_general(wr_ref[...].astype(BF16), h2.astype(BF16),
                             (((1,), (1,)), ((), ())),
                             preferred_element_type=F32) + br_ref[...]
    e_iota = lax.broadcasted_iota(jnp.int32, (N_EXPERTS, T), 0)
    work = logits
    vals, idxs, sels = [], [], []
    for _k in range(TOP_K):
        m = jnp.max(work, axis=0, keepdims=True)
        am = jnp.min(jnp.where(work == m, e_iota, N_EXPERTS), axis=0, keepdims=True)
        sel = e_iota == am
        vals.append(m)
        idxs.append(am)
        sels.append(sel)
        work = jnp.where(sel, -jnp.inf, work)
    exps = [jnp.exp(v - vals[0]) for v in vals]
    denom = exps[0] + exps[1] + exps[2] + exps[3]
    gate_ref[...] = jnp.concatenate([e / denom for e in exps], axis=0)
    idx_ref[...] = jnp.concatenate(idxs, axis=0)

    member = jnp.zeros((N_EXPERTS, T), F32)
    for sel in sels:
        member = member + jnp.where(sel, 1.0, 0.0)
    r_i = lax.broadcasted_iota(jnp.int32, (T, T), 0)
    c_i = lax.broadcasted_iota(jnp.int32, (T, T), 1)
    upper = jnp.where(r_i < c_i, 1.0, 0.0).astype(BF16)
    before = jnp.dot(member.astype(BF16), upper, preferred_element_type=F32)
    before = before + cnt_s[:, 0:1]
    ranks = [jnp.sum(jnp.where(sel, before, 0.0), axis=0, keepdims=True) for sel in sels]
    rank_ref[...] = jnp.concatenate(ranks, axis=0).astype(jnp.int32)
    cnt_new = cnt_s[...] + jnp.sum(member, axis=1, keepdims=True)
    cnt_s[...] = cnt_new
    cnt_ref[...] = cnt_new.astype(jnp.int32)


def _dispatch_kernel(dest_ref, padlo_ref, padn_ref, tot_ref, h2_ref, xs_ref, zbuf, sem, zsem, tsem):
    i = pl.program_id(0)
    blk_lines = EXPERT_BLOCK * TOKEN_LINES
    n_blocks = xs_ref.shape[0] // blk_lines

    def pad_pieces(e, op):
        n = padn_ref[e]
        line = padlo_ref[e]
        for bit in PAD_BITS:
            hit = (n & bit) != 0

            @pl.when(hit)
            def _(line=line, bit=bit):
                op(pltpu.make_async_copy(
                    zbuf.at[pl.ds(0, bit * TOKEN_LINES)],
                    xs_ref.at[pl.ds(pl.multiple_of(line, TOKEN_LINES), bit * TOKEN_LINES)], zsem))

            line = line + jnp.where(hit, bit * TOKEN_LINES, 0)

    def for_each_expert(op):
        def body(e, carry):
            pad_pieces(e, op)
            return carry
        lax.fori_loop(0, N_EXPERTS, body, 0)

    @pl.when(i == 0)
    def _():
        zbuf[...] = jnp.zeros_like(zbuf)
        for_each_expert(lambda c: c.start())

    tail_blk = tot_ref[0] + i

    def tail_copy():
        line = pl.multiple_of(tail_blk * blk_lines, blk_lines)
        return pltpu.make_async_copy(zbuf, xs_ref.at[pl.ds(line, blk_lines)], tsem)

    @pl.when(tail_blk < n_blocks)
    def _():
        tail_copy().start()

    def start(g, carry):
        for j in range(ISSUE_UNROLL):
            t = g * ISSUE_UNROLL + j
            src = h2_ref.at[pl.ds(pl.multiple_of(t * TOKEN_LINES, TOKEN_LINES), TOKEN_LINES)]
            for k in range(TOP_K):
                line = pl.multiple_of(dest_ref[k, t], TOKEN_LINES)
                pltpu.make_async_copy(src, xs_ref.at[pl.ds(line, TOKEN_LINES)], sem).start(priority=k % 2)
        return carry

    lax.fori_loop(0, DISPATCH_TILE // ISSUE_UNROLL, start, 0)
    for k in range(TOP_K):
        pltpu.make_async_copy(h2_ref, xs_ref.at[pl.ds(0, DISPATCH_TILE * TOKEN_LINES)], sem).wait()

    @pl.when(i == 0)
    def _():
        for_each_expert(lambda c: c.wait())

    @pl.when(tail_blk < n_blocks)
    def _():
        tail_copy().wait()


def _expert_kernel(be_ref, tot_ref, xs_ref, w1_ref, b1_ref, w2_ref, b2_ref, y_ref, w1b, w2b):
    i = pl.program_id(0)

    @pl.when(i < tot_ref[0])
    def _():
        @pl.when((i == 0) | (be_ref[i] != be_ref[jnp.maximum(i - 1, 0)]))
        def _():
            w1b[...] = w1_ref[...].astype(BF16)
            w2b[...] = w2_ref[...].astype(BF16)

        xb = _load_token_tiles(xs_ref, EXPERT_BLOCK).astype(BF16)
        hcat = jnp.dot(xb, w1b[...], preferred_element_type=F32) + b1_ref[...]
        x_glu = jnp.minimum(hcat[:, :D_FF], SWIGLU_LIMIT)
        x_lin = jnp.clip(hcat[:, D_FF:], -SWIGLU_LIMIT, SWIGLU_LIMIT)
        act = x_glu * jax.nn.sigmoid(SWIGLU_ALPHA * x_glu) * (x_lin + 1.0)
        y = jnp.dot(act.astype(BF16), w2b[...], preferred_element_type=F32) + b2_ref[...]
        _store_token_tiles(y_ref, y, EXPERT_BLOCK)

    @pl.when(i >= tot_ref[0])
    def _():
        y_ref[...] = jnp.zeros_like(y_ref)


def _combine_kernel(dest_ref, dnext_ref, x1_ref, gate_ref, fg_ref, y_ref, out_ref, buf, sem):
    TC = COMBINE_TILE
    i = pl.program_id(0)
    cur = i % 2
    nxt = (i + 1) % 2

    def issue(d_ref, slot):
        def start(g, carry):
            for j in range(ISSUE_UNROLL):
                t = g * ISSUE_UNROLL + j
                row = t * GATHER_PITCH
                for k in range(TOP_K):
                    line = pl.multiple_of(d_ref[k, t], TOKEN_LINES)
                    pltpu.make_async_copy(
                        y_ref.at[pl.ds(line, TOKEN_LINES)],
                        buf.at[slot, k, pl.ds(row, TOKEN_LINES)],
                        sem.at[slot]).start(priority=k % 2)
            return carry

        lax.fori_loop(0, TC // ISSUE_UNROLL, start, 0)

    @pl.when(i == 0)
    def _():
        issue(dest_ref, cur)

    @pl.when(i + 1 < pl.num_programs(0))
    def _():
        issue(dnext_ref, nxt)

    for k in range(TOP_K):
        pltpu.make_async_copy(y_ref.at[pl.ds(0, TC * TOKEN_LINES)],
                              buf.at[cur, k, pl.ds(0, TC * TOKEN_LINES)], sem.at[cur]).wait()

    gpad = jnp.concatenate([gate_ref[...], jnp.zeros((128 - TOP_K, TC), F32)], axis=0)
    gt = gpad.T
    acc = x1_ref[...]
    for k in range(TOP_K):
        acc = acc + gt[:, k:k + 1] * _load_token_tiles(buf, TC, lead=(cur, k), pitch=GATHER_PITCH)
    out_ref[...] = _rms(acc, fg_ref[...])


def _full(shape):
    n = len(shape)
    return pl.BlockSpec(shape, lambda *_: (0,) * n)


def kernel(x, norm1_g, w_in, b_in, conv_w, conv_b, conv_ln_g, conv_ln_b, sg_ln_g, sg_ln_b,
           sg_w, sg_b, grp_g_conv, grp_g_sg, w_out, b_out, norm2_g, w_router, b_router,
           w_exp1, b_exp1, w_exp2, b_exp2, final_g):
    B, S, D = x.shape
    assert D == D_MODEL and S % MIX_TILE == 0 and norm1_g.shape[0] == 1
    N = B * S
    T = MIX_TILE
    n_s = S // T
    l = 0

    row = lambda v: v.reshape(1, -1).astype(F32)
    cw = jnp.concatenate([conv_w[l], jnp.zeros((1, D_CONV), F32)], axis=0)
    sgw = sg_w[l].reshape(GMLP_HEADS // 2, 2, GMLP_BLOCK, GMLP_BLOCK)
    sgw = sgw.transpose(0, 2, 1, 3).reshape(GMLP_HEADS // 2, GMLP_BLOCK, 2 * GMLP_BLOCK)
    sgb = jnp.repeat(sg_b[l].T, GMLP_HEAD_DIM, axis=1)
    wr_t = w_router[l].T
    br = b_router[l].reshape(N_EXPERTS, 1)

    mix_in = [
        x, row(norm1_g[l]), w_in[l].astype(BF16), row(b_in[l]), cw, row(conv_b[l]),
        row(conv_ln_g[l]), row(conv_ln_b[l]), row(sg_ln_g[l]), row(sg_ln_b[l]), sgw, sgb,
        row(grp_g_conv[l]), row(grp_g_sg[l]), w_out[l].astype(BF16), row(b_out[l]),
        row(norm2_g[l]), wr_t, br,
    ]
    mix_specs = [pl.BlockSpec((None, T, D), lambda b, s: (b, s, 0))]
    mix_specs += [_full(a.shape) for a in mix_in[1:]]
    tok_spec = pl.BlockSpec((TOP_K, T), lambda b, s: (0, b * n_s + s))
    x1, h2, idx, gate, rank, cnt = pl.pallas_call(
        _mix_kernel,
        grid=(B, n_s),
        in_specs=mix_specs,
        out_specs=[
            pl.BlockSpec((None, T, D), lambda b, s: (b, s, 0)),
            pl.BlockSpec((T * TOKEN_LINES, LANES), lambda b, s: (b * n_s + s, 0)),
            tok_spec, tok_spec, tok_spec,
            pl.BlockSpec((N_EXPERTS, 128), lambda b, s: (0, 0)),
        ],
        out_shape=[
            jax.ShapeDtypeStruct((B, S, D), F32),
            jax.ShapeDtypeStruct((N * TOKEN_LINES, LANES), F32),
            jax.ShapeDtypeStruct((TOP_K, N), jnp.int32),
            jax.ShapeDtypeStruct((TOP_K, N), F32),
            jax.ShapeDtypeStruct((TOP_K, N), jnp.int32),
            jax.ShapeDtypeStruct((N_EXPERTS, 128), jnp.int32),
        ],
        scratch_shapes=[
            pltpu.VMEM((T + CONV_HIST, D_CONV), F32),
            pltpu.VMEM((SUBLANES - 1, T + CONV_HIST, D_CONV), F32),
            pltpu.VMEM((T, D_MODEL), BF16),
            pltpu.VMEM((N_EXPERTS, 128), F32),
        ],
        compiler_params=pltpu.CompilerParams(
            dimension_semantics=("arbitrary", "arbitrary"), vmem_limit_bytes=VMEM_LIMIT),
        name="mix",
    )(*mix_in)

    blk = EXPERT_BLOCK
    n_blocks = (N * TOP_K) // blk + N_EXPERTS
    n_slots = n_blocks * blk
    counts = cnt[:, 0]
    nblk_e = (counts + blk - 1) // blk
    block_end = jnp.cumsum(nblk_e)
    pad_start = (block_end - nblk_e) * blk
    dest = rank
    for e in range(N_EXPERTS):
        dest = dest + jnp.where(idx == e, pad_start[e], 0)
    dest = dest * TOKEN_LINES
    blk_ids = jnp.arange(n_blocks, dtype=jnp.int32)
    block_e = jnp.minimum(
        jnp.sum((blk_ids[:, None] >= block_end[None, :]).astype(jnp.int32), axis=1),
        N_EXPERTS - 1).astype(jnp.int32)
    total = block_end[-1:].astype(jnp.int32)
    pad_lo = ((pad_start + counts) * TOKEN_LINES).astype(jnp.int32)
    pad_n = (nblk_e * blk - counts).astype(jnp.int32)
    assert N // DISPATCH_TILE >= N_EXPERTS

    xs = pl.pallas_call(
        _dispatch_kernel,
        grid=(N // DISPATCH_TILE,),
        in_specs=[
            pl.BlockSpec((TOP_K, DISPATCH_TILE), lambda i: (0, i), memory_space=pltpu.SMEM),
            pl.BlockSpec(memory_space=pltpu.SMEM),
            pl.BlockSpec(memory_space=pltpu.SMEM),
            pl.BlockSpec(memory_space=pltpu.SMEM),
            pl.BlockSpec((DISPATCH_TILE * TOKEN_LINES, LANES), lambda i: (i, 0)),
        ],
        out_specs=pl.BlockSpec(memory_space=pl.ANY),
        out_shape=jax.ShapeDtypeStruct((n_slots * TOKEN_LINES, LANES), F32),
        scratch_shapes=[pltpu.VMEM((blk * TOKEN_LINES, LANES), F32),
                        pltpu.SemaphoreType.DMA(()), pltpu.SemaphoreType.DMA(()),
                        pltpu.SemaphoreType.DMA(())],
        compiler_params=pltpu.CompilerParams(
            dimension_semantics=("arbitrary",), vmem_limit_bytes=VMEM_LIMIT),
        name="dispatch",
    )(dest, pad_lo, pad_n, total, h2)

    def blk_map(i, be, tot):
        return (jnp.minimum(i, tot[0] - 1), 0)

    def exp_map(i, be, tot):
        return (be[jnp.minimum(i, tot[0] - 1)], 0, 0)

    y = pl.pallas_call(
        _expert_kernel,
        grid_spec=pltpu.PrefetchScalarGridSpec(
            num_scalar_prefetch=2,
            grid=(n_blocks,),
            in_specs=[
                pl.BlockSpec((blk * TOKEN_LINES, LANES), blk_map),
                pl.BlockSpec((None, D, 2 * D_FF), exp_map),
                pl.BlockSpec((None, 1, 2 * D_FF), exp_map),
                pl.BlockSpec((None, D_FF, D), exp_map),
                pl.BlockSpec((None, 1, D), exp_map),
            ],
            out_specs=pl.BlockSpec((blk * TOKEN_LINES, LANES), lambda i, be, tot: (i, 0)),
            scratch_shapes=[pltpu.VMEM((D, 2 * D_FF), BF16), pltpu.VMEM((D_FF, D), BF16)],
        ),
        out_shape=jax.ShapeDtypeStruct((n_slots * TOKEN_LINES, LANES), F32),
        compiler_params=pltpu.CompilerParams(
            dimension_semantics=("arbitrary",), vmem_limit_bytes=VMEM_LIMIT),
        name="experts",
    )(block_e, total, xs, w_exp1[l], b_exp1[l].reshape(N_EXPERTS, 1, 2 * D_FF),
      w_exp2[l], b_exp2[l].reshape(N_EXPERTS, 1, D))

    TC = COMBINE_TILE
    out = pl.pallas_call(
        _combine_kernel,
        grid=(N // TC,),
        in_specs=[
            pl.BlockSpec((TOP_K, TC), lambda i: (0, i), memory_space=pltpu.SMEM),
            pl.BlockSpec((TOP_K, TC), lambda i: (0, jnp.minimum(i + 1, N // TC - 1)),
                         memory_space=pltpu.SMEM),
            pl.BlockSpec((TC, D), lambda i: (i, 0)),
            pl.BlockSpec((TOP_K, TC), lambda i: (0, i)),
            _full((1, D)),
            pl.BlockSpec(memory_space=pl.ANY),
        ],
        out_specs=pl.BlockSpec((TC, D), lambda i: (i, 0)),
        out_shape=jax.ShapeDtypeStruct((N, D), F32),
        scratch_shapes=[pltpu.VMEM((2, TOP_K, TC * GATHER_PITCH, LANES), F32),
                        pltpu.SemaphoreType.DMA((2,))],
        compiler_params=pltpu.CompilerParams(
            dimension_semantics=("arbitrary",), vmem_limit_bytes=VMEM_LIMIT),
        name="combine",
    )(dest, dest, x1.reshape(N, D), gate, row(final_g), y)
    return out.reshape(B, S, D)
```

```python
import functools

import jax
import jax.numpy as jnp
from jax import lax
from jax.experimental import pallas as pl
from jax.experimental.pallas import tpu as pltpu

D_MODEL = 1024
D_CONV = 512
D_GMLP = 512
CONV_WIDTH = 31
CONV_HIST = 32
GMLP_HEADS = 8
GMLP_HEAD_DIM = 64
GMLP_BLOCK = 128
CHUNK = 64
N_EXPERTS = 32
TOP_K = 4
D_FF = 1024
SWIGLU_ALPHA = 1.702
SWIGLU_LIMIT = 7.0
EPS = 1e-5

MIX_TILE = 512
CONV_ROWS = 32
SUBLANES = 8
SHIFT_ROWS = 136
EXPERT_BLOCK = 512
DISPATCH_TILE = 1024
COMBINE_TILE = 256
ISSUE_UNROLL = 8
PAD_BITS = tuple(1 << b for b in reversed(range(EXPERT_BLOCK.bit_length() - 1)))
VMEM_LIMIT = 56 * 1024 * 1024

F32 = jnp.float32
BF16 = jnp.bfloat16


def _rms(x, g):
    return x * lax.rsqrt(jnp.mean(x * x, axis=-1, keepdims=True) + EPS) * g


def _ln(x, g, b):
    mu = jnp.mean(x, axis=-1, keepdims=True)
    xc = x - mu
    var = jnp.mean(xc * xc, axis=-1, keepdims=True)
    return xc * lax.rsqrt(var + EPS) * g + b


def _gelu(x):
    return 0.5 * x * (1.0 + lax.erf(x * (2.0 ** -0.5)))


LANES = 128
TOKEN_LINES = D_MODEL // LANES


GATHER_PITCH = TOKEN_LINES + 1


def _load_token_tiles(ref, rows, lead=(), pitch=TOKEN_LINES):
    chunks = [ref[lead + (pl.ds(c, rows, stride=pitch), slice(None))] for c in range(TOKEN_LINES)]
    return jnp.concatenate(chunks, axis=1)


def _store_token_tiles(ref, value, rows):
    for c in range(TOKEN_LINES):
        ref[pl.ds(c, rows, stride=TOKEN_LINES), :] = value[:, c * LANES:(c + 1) * LANES]


def _mix_kernel(x_ref, g1_ref, win_ref, bin_ref, cw_ref, cb_ref, clg_ref, clb_ref,
                slg_ref, slb_ref, sgw_ref, sgb_ref, ggc_ref, ggs_ref, wout_ref, bout_ref,
                g2_ref, wr_ref, br_ref,
                x1_ref, h2_ref, idx_ref, gate_ref, rank_ref, cnt_ref,
                ubuf, ushift, ycat, cnt_s):
    T = MIX_TILE
    b = pl.program_id(0)
    s = pl.program_id(1)

    @pl.when(s == 0)
    def _():
        ubuf[0:CONV_HIST, :] = jnp.zeros((CONV_HIST, D_CONV), F32)

    @pl.when((b == 0) & (s == 0))
    def _():
        cnt_s[...] = jnp.zeros_like(cnt_s)

    x = x_ref[...]
    h = _rms(x, g1_ref[...])
    p = jnp.dot(h.astype(BF16), win_ref[...], preferred_element_type=F32) + bin_ref[...]

    u = p[:, 0:D_CONV] * jax.nn.sigmoid(p[:, D_CONV:2 * D_CONV])
    ubuf[CONV_HIST:CONV_HIST + T, :] = u

    n_rows = T + CONV_HIST
    for r in range(1, SUBLANES):
        ushift[r - 1] = pltpu.roll(ubuf[...], n_rows - r, axis=0)

    for c in range(T // CONV_ROWS):
        r0 = c * CONV_ROWS
        acc = jnp.broadcast_to(cb_ref[...], (CONV_ROWS, D_CONV))
        for k in range(CONV_WIDTH):
            off = CONV_HIST - (CONV_WIDTH - 1) + k
            q, r = divmod(off, SUBLANES)
            a0 = r0 + q * SUBLANES
            if r == 0:
                tap = ubuf[a0:a0 + CONV_ROWS, :]
            else:
                tap = ushift[r - 1, a0:a0 + CONV_ROWS, :]
            w8 = cw_ref[k * SUBLANES:(k + 1) * SUBLANES, :]
            acc = acc + jnp.concatenate([w8] * (CONV_ROWS // SUBLANES), axis=0) * tap
        y = _ln(acc, clg_ref[...], clb_ref[...])
        y = y * jax.nn.sigmoid(y)
        y = _rms(y, ggc_ref[...])
        ycat[r0:r0 + CONV_ROWS, 0:D_CONV] = y.astype(BF16)
    ubuf[0:CONV_HIST, :] = ubuf[T:T + CONV_HIST, :]

    uu = _gelu(p[:, 2 * D_CONV:2 * D_CONV + D_GMLP])
    vv = _ln(_gelu(p[:, 2 * D_CONV + D_GMLP:]), slg_ref[...], slb_ref[...])
    t_out = lax.broadcasted_iota(jnp.int32, (GMLP_BLOCK, 2 * GMLP_BLOCK), 0)
    s_in = lax.broadcasted_iota(jnp.int32, (GMLP_BLOCK, 2 * GMLP_BLOCK), 1) % GMLP_BLOCK
    wmask = (s_in // CHUNK) <= (t_out // CHUNK)
    lane = lax.broadcasted_iota(jnp.int32, (GMLP_BLOCK, 2 * GMLP_HEAD_DIM), 1)
    lo = lane < GMLP_HEAD_DIM
    n_pair = GMLP_HEADS // 2
    wcat = [jnp.where(wmask, sgw_ref[j], 0.0).astype(BF16) for j in range(n_pair)]
    for blk in range(T // GMLP_BLOCK):
        rows = slice(blk * GMLP_BLOCK, (blk + 1) * GMLP_BLOCK)
        outs = []
        for j in range(n_pair):
            cols = slice(j * 128, (j + 1) * 128)
            vp = vv[rows, cols]
            vstack = jnp.concatenate(
                [jnp.where(lo, vp, 0.0), jnp.where(lo, 0.0, vp)], axis=0).astype(BF16)
            mixed = jnp.dot(wcat[j], vstack, preferred_element_type=F32) + sgb_ref[:, cols]
            outs.append(uu[rows, cols] * mixed)
        ysg = jnp.concatenate(outs, axis=1)
        ycat[rows, D_CONV:] = _rms(ysg, ggs_ref[...]).astype(BF16)

    o = jnp.dot(ycat[...], wout_ref[...], preferred_element_type=F32) + bout_ref[...]
    x1 = x + o
    x1_ref[...] = x1
    h2 = _rms(x1, g2_ref[...])
    _store_token_tiles(h2_ref, h2, T)

    logits = lax.dot_general(wr_ref[...].astype(BF16), h2.astype(BF16),
                             (((1,), (1,)), ((), ())),
                             preferred_element_type=F32) + br_ref[...]
    e_iota = lax.broadcasted_iota(jnp.int32, (N_EXPERTS, T), 0)
    work = logits
    vals, idxs, sels = [], [], []
    for _k in range(TOP_K):
        m = jnp.max(work, axis=0, keepdims=True)
        am = jnp.min(jnp.where(work == m, e_iota, N_EXPERTS), axis=0, keepdims=True)
        sel = e_iota == am
        vals.append(m)
        idxs.append(am)
        sels.append(sel)
        work = jnp.where(sel, -jnp.inf, work)
    exps = [jnp.exp(v - vals[0]) for v in vals]
    denom = exps[0] + exps[1] + exps[2] + exps[3]
    gate_ref[...] = jnp.concatenate([e / denom for e in exps], axis=0)
    idx_ref[...] = jnp.concatenate(idxs, axis=0)

    member = jnp.zeros((N_EXPERTS, T), F32)
    for sel in sels:
        member = member + jnp.where(sel, 1.0, 0.0)
    r_i = lax.broadcasted_iota(jnp.int32, (T, T), 0)
    c_i = lax.broadcasted_iota(jnp.int32, (T, T), 1)
    upper = jnp.where(r_i < c_i, 1.0, 0.0).astype(BF16)
    before = jnp.dot(member.astype(BF16), upper, preferred_element_type=F32)
    before = before + cnt_s[:, 0:1]
    ranks = [jnp.sum(jnp.where(sel, before, 0.0), axis=0, keepdims=True) for sel in sels]
    rank_ref[...] = jnp.concatenate(ranks, axis=0).astype(jnp.int32)
    cnt_new = cnt_s[...] + jnp.sum(member, axis=1, keepdims=True)
    cnt_s[...] = cnt_new
    cnt_ref[...] = cnt_new.astype(jnp.int32)


def _dispatch_kernel(dest_ref, padlo_ref, padn_ref, tot_ref, h2_ref, xs_ref, zbuf, sem, zsem, tsem):
    i = pl.program_id(0)
    blk_lines = EXPERT_BLOCK * TOKEN_LINES
    n_blocks = xs_ref.shape[0] // blk_lines

    def pad_pieces(e, op):
        n = padn_ref[e]
        line = padlo_ref[e]
        for bit in PAD_BITS:
            hit = (n & bit) != 0

            @pl.when(hit)
            def _(line=line, bit=bit):
                op(pltpu.make_async_copy(
                    zbuf.at[pl.ds(0, bit * TOKEN_LINES)],
                    xs_ref.at[pl.ds(pl.multiple_of(line, TOKEN_LINES), bit * TOKEN_LINES)], zsem))

            line = line + jnp.where(hit, bit * TOKEN_LINES, 0)

    def for_each_expert(op):
        def body(e, carry):
            pad_pieces(e, op)
            return carry
        lax.fori_loop(0, N_EXPERTS, body, 0)

    @pl.when(i == 0)
    def _():
        zbuf[...] = jnp.zeros_like(zbuf)
        for_each_expert(lambda c: c.start())

    tail_blk = tot_ref[0] + i

    def tail_copy():
        line = pl.multiple_of(tail_blk * blk_lines, blk_lines)
        return pltpu.make_async_copy(zbuf, xs_ref.at[pl.ds(line, blk_lines)], tsem)

    @pl.when(tail_blk < n_blocks)
    def _():
        tail_copy().start()

    def start(g, carry):
        for j in range(ISSUE_UNROLL):
            t = g * ISSUE_UNROLL + j
            src = h2_ref.at[pl.ds(pl.multiple_of(t * TOKEN_LINES, TOKEN_LINES), TOKEN_LINES)]
            for k in range(TOP_K):
                line = pl.multiple_of(dest_ref[k, t], TOKEN_LINES)
                pltpu.make_async_copy(src, xs_ref.at[pl.ds(line, TOKEN_LINES)], sem).start(priority=k % 2)
        return carry

    lax.fori_loop(0, DISPATCH_TILE // ISSUE_UNROLL, start, 0)
    for k in range(TOP_K):
        pltpu.make_async_copy(h2_ref, xs_ref.at[pl.ds(0, DISPATCH_TILE * TOKEN_LINES)], sem).wait()

    @pl.when(i == 0)
    def _():
        for_each_expert(lambda c: c.wait())

    @pl.when(tail_blk < n_blocks)
    def _():
        tail_copy().wait()


def _expert_kernel(be_ref, tot_ref, xs_ref, w1_ref, b1_ref, w2_ref, b2_ref, y_ref, w1b, w2b):
    i = pl.program_id(0)

    @pl.when(i < tot_ref[0])
    def _():
        @pl.when((i == 0) | (be_ref[i] != be_ref[jnp.maximum(i - 1, 0)]))
        def _():
            w1b[...] = w1_ref[...].astype(BF16)
            w2b[...] = w2_ref[...].astype(BF16)

        xb = _load_token_tiles(xs_ref, EXPERT_BLOCK).astype(BF16)
        hcat = jnp.dot(xb, w1b[...], preferred_element_type=F32) + b1_ref[...]
        x_glu = jnp.minimum(hcat[:, :D_FF], SWIGLU_LIMIT)
        x_lin = jnp.clip(hcat[:, D_FF:], -SWIGLU_LIMIT, SWIGLU_LIMIT)
        act = x_glu * jax.nn.sigmoid(SWIGLU_ALPHA * x_glu) * (x_lin + 1.0)
        y = jnp.dot(act.astype(BF16), w2b[...], preferred_element_type=F32) + b2_ref[...]
        _store_token_tiles(y_ref, y, EXPERT_BLOCK)

    @pl.when(i >= tot_ref[0])
    def _():
        y_ref[...] = jnp.zeros_like(y_ref)


def _combine_kernel(dest_ref, dnext_ref, x1_ref, gate_ref, fg_ref, y_ref, out_ref, buf, sem):
    TC = COMBINE_TILE
    i = pl.program_id(0)
    cur = i % 2
    nxt = (i + 1) % 2

    def issue(d_ref, slot):
        def start(g, carry):
            for j in range(ISSUE_UNROLL):
                t = g * ISSUE_UNROLL + j
                row = t * GATHER_PITCH
                for k in range(TOP_K):
                    line = pl.multiple_of(d_ref[k, t], TOKEN_LINES)
                    pltpu.make_async_copy(
                        y_ref.at[pl.ds(line, TOKEN_LINES)],
                        buf.at[slot, k, pl.ds(row, TOKEN_LINES)],
                        sem.at[slot]).start(priority=k % 2)
            return carry

        lax.fori_loop(0, TC // ISSUE_UNROLL, start, 0)

    @pl.when(i == 0)
    def _():
        issue(dest_ref, cur)

    @pl.when(i + 1 < pl.num_programs(0))
    def _():
        issue(dnext_ref, nxt)

    for k in range(TOP_K):
        pltpu.make_async_copy(y_ref.at[pl.ds(0, TC * TOKEN_LINES)],
                              buf.at[cur, k, pl.ds(0, TC * TOKEN_LINES)], sem.at[cur]).wait()

    gpad = jnp.concatenate([gate_ref[...], jnp.zeros((128 - TOP_K, TC), F32)], axis=0)
    gt = gpad.T
    acc = x1_ref[...]
    for k in range(TOP_K):
        acc = acc + gt[:, k:k + 1] * _load_token_tiles(buf, TC, lead=(cur, k), pitch=GATHER_PITCH)
    out_ref[...] = _rms(acc, fg_ref[...])


def _full(shape):
    n = len(shape)
    return pl.BlockSpec(shape, lambda *_: (0,) * n)


def kernel(x, norm1_g, w_in, b_in, conv_w, conv_b, conv_ln_g, conv_ln_b, sg_ln_g, sg_ln_b,
           sg_w, sg_b, grp_g_conv, grp_g_sg, w_out, b_out, norm2_g, w_router, b_router,
           w_exp1, b_exp1, w_exp2, b_exp2, final_g):
    B, S, D = x.shape
    assert D == D_MODEL and S % MIX_TILE == 0 and norm1_g.shape[0] == 1
    N = B * S
    T = MIX_TILE
    n_s = S // T
    l = 0

    row = lambda v: v.reshape(1, -1).astype(F32)
    cw = jnp.repeat(conv_w[l].astype(F32), SUBLANES, axis=0)
    sgw = sg_w[l].reshape(GMLP_HEADS // 2, 2, GMLP_BLOCK, GMLP_BLOCK)
    sgw = sgw.transpose(0, 2, 1, 3).reshape(GMLP_HEADS // 2, GMLP_BLOCK, 2 * GMLP_BLOCK)
    sgb = jnp.repeat(sg_b[l].T, GMLP_HEAD_DIM, axis=1)
    wr_t = w_router[l].T
    br = b_router[l].reshape(N_EXPERTS, 1)

    mix_in = [
        x, row(norm1_g[l]), w_in[l].astype(BF16), row(b_in[l]), cw, row(conv_b[l]),
        row(conv_ln_g[l]), row(conv_ln_b[l]), row(sg_ln_g[l]), row(sg_ln_b[l]), sgw, sgb,
        row(grp_g_conv[l]), row(grp_g_sg[l]), w_out[l].astype(BF16), row(b_out[l]),
        row(norm2_g[l]), wr_t, br,
    ]
    mix_specs = [pl.BlockSpec((None, T, D), lambda b, s: (b, s, 0))]
    mix_specs += [_full(a.shape) for a in mix_in[1:]]
    tok_spec = pl.BlockSpec((TOP_K, T), lambda b, s: (0, b * n_s + s))
    x1, h2, idx, gate, rank, cnt = pl.pallas_call(
        _mix_kernel,
        grid=(B, n_s),
        in_specs=mix_specs,
        out_specs=[
            pl.BlockSpec((None, T, D), lambda b, s: (b, s, 0)),
            pl.BlockSpec((T * TOKEN_LINES, LANES), lambda b, s: (b * n_s + s, 0)),
            tok_spec, tok_spec, tok_spec,
            pl.BlockSpec((N_EXPERTS, 128), lambda b, s: (0, 0)),
        ],
        out_shape=[
            jax.ShapeDtypeStruct((B, S, D), F32),
            jax.ShapeDtypeStruct((N * TOKEN_LINES, LANES), F32),
            jax.ShapeDtypeStruct((TOP_K, N), jnp.int32),
            jax.ShapeDtypeStruct((TOP_K, N), F32),
            jax.ShapeDtypeStruct((TOP_K, N), jnp.int32),
            jax.ShapeDtypeStruct((N_EXPERTS, 128), jnp.int32),
        ],
        scratch_shapes=[
            pltpu.VMEM((T + CONV_HIST, D_CONV), F32),
            pltpu.VMEM((SUBLANES - 1, T + CONV_HIST, D_CONV), F32),
            pltpu.VMEM((T, D_MODEL), BF16),
            pltpu.VMEM((N_EXPERTS, 128), F32),
        ],
        compiler_params=pltpu.CompilerParams(
            dimension_semantics=("arbitrary", "arbitrary"), vmem_limit_bytes=VMEM_LIMIT),
        name="mix",
    )(*mix_in)

    blk = EXPERT_BLOCK
    n_blocks = (N * TOP_K) // blk + N_EXPERTS
    n_slots = n_blocks * blk
    counts = cnt[:, 0]
    nblk_e = (counts + blk - 1) // blk
    block_end = jnp.cumsum(nblk_e)
    pad_start = (block_end - nblk_e) * blk
    dest = rank
    for e in range(N_EXPERTS):
        dest = dest + jnp.where(idx == e, pad_start[e], 0)
    dest = dest * TOKEN_LINES
    blk_ids = jnp.arange(n_blocks, dtype=jnp.int32)
    block_e = jnp.minimum(
        jnp.sum((blk_ids[:, None] >= block_end[None, :]).astype(jnp.int32), axis=1),
        N_EXPERTS - 1).astype(jnp.int32)
    total = block_end[-1:].astype(jnp.int32)
    pad_lo = ((pad_start + counts) * TOKEN_LINES).astype(jnp.int32)
    pad_n = (nblk_e * blk - counts).astype(jnp.int32)
    assert N // DISPATCH_TILE >= N_EXPERTS

    xs = pl.pallas_call(
        _dispatch_kernel,
        grid=(N // DISPATCH_TILE,),
        in_specs=[
            pl.BlockSpec((TOP_K, DISPATCH_TILE), lambda i: (0, i), memory_space=pltpu.SMEM),
            pl.BlockSpec(memory_space=pltpu.SMEM),
            pl.BlockSpec(memory_space=pltpu.SMEM),
            pl.BlockSpec(memory_space=pltpu.SMEM),
            pl.BlockSpec((DISPATCH_TILE * TOKEN_LINES, LANES), lambda i: (i, 0)),
        ],
        out_specs=pl.BlockSpec(memory_space=pl.ANY),
        out_shape=jax.ShapeDtypeStruct((n_slots * TOKEN_LINES, LANES), F32),
        scratch_shapes=[pltpu.VMEM((blk * TOKEN_LINES, LANES), F32),
                        pltpu.SemaphoreType.DMA(()), pltpu.SemaphoreType.DMA(()),
                        pltpu.SemaphoreType.DMA(())],
        compiler_params=pltpu.CompilerParams(
            dimension_semantics=("arbitrary",), vmem_limit_bytes=VMEM_LIMIT),
        name="dispatch",
    )(dest, pad_lo, pad_n, total, h2)

    def blk_map(i, be, tot):
        return (jnp.minimum(i, tot[0] - 1), 0)

    def exp_map(i, be, tot):
        return (be[jnp.minimum(i, tot[0] - 1)], 0, 0)

    y = pl.pallas_call(
        _expert_kernel,
        grid_spec=pltpu.PrefetchScalarGridSpec(
            num_scalar_prefetch=2,
            grid=(n_blocks,),
            in_specs=[
                pl.BlockSpec((blk * TOKEN_LINES, LANES), blk_map),
                pl.BlockSpec((None, D, 2 * D_FF), exp_map),
                pl.BlockSpec((None, 1, 2 * D_FF), exp_map),
                pl.BlockSpec((None, D_FF, D), exp_map),
                pl.BlockSpec((None, 1, D), exp_map),
            ],
            out_specs=pl.BlockSpec((blk * TOKEN_LINES, LANES), lambda i, be, tot: (i, 0)),
            scratch_shapes=[pltpu.VMEM((D, 2 * D_FF), BF16), pltpu.VMEM((D_FF, D), BF16)],
        ),
        out_shape=jax.ShapeDtypeStruct((n_slots * TOKEN_LINES, LANES), F32),
        compiler_params=pltpu.CompilerParams(
            dimension_semantics=("arbitrary",), vmem_limit_bytes=VMEM_LIMIT),
        name="experts",
    )(block_e, total, xs, w_exp1[l], b_exp1[l].reshape(N_EXPERTS, 1, 2 * D_FF),
      w_exp2[l], b_exp2[l].reshape(N_EXPERTS, 1, D))

    TC = COMBINE_TILE
    out = pl.pallas_call(
        _combine_kernel,
        grid=(N // TC,),
        in_specs=[
            pl.BlockSpec((TOP_K, TC), lambda i: (0, i), memory_space=pltpu.SMEM),
            pl.BlockSpec((TOP_K, TC), lambda i: (0, jnp.minimum(i + 1, N // TC - 1)),
                         memory_space=pltpu.SMEM),
            pl.BlockSpec((TC, D), lambda i: (i, 0)),
            pl.BlockSpec((TOP_K, TC), lambda i: (0, i)),
            _full((1, D)),
            pl.BlockSpec(memory_space=pl.ANY),
        ],
        out_specs=pl.BlockSpec((TC, D), lambda i: (i, 0)),
        out_shape=jax.ShapeDtypeStruct((N, D), F32),
        scratch_shapes=[pltpu.VMEM((2, TOP_K, TC * GATHER_PITCH, LANES), F32),
                        pltpu.SemaphoreType.DMA((2,))],
        compiler_params=pltpu.CompilerParams(
            dimension_semantics=("arbitrary",), vmem_limit_bytes=VMEM_LIMIT),
        name="combine",
    )(dest, dest, x1.reshape(N, D), gate, row(final_g), y)
    return out.reshape(B, S, D)
```

```python
import functools

import jax
import jax.numpy as jnp
from jax import lax
from jax.experimental import pallas as pl
from jax.experimental.pallas import tpu as pltpu

D_MODEL = 1024
D_CONV = 512
D_GMLP = 512
CONV_WIDTH = 31
CONV_HIST = 32
GMLP_HEADS = 8
GMLP_HEAD_DIM = 64
GMLP_BLOCK = 128
CHUNK = 64
N_EXPERTS = 32
TOP_K = 4
D_FF = 1024
SWIGLU_ALPHA = 1.702
SWIGLU_LIMIT = 7.0
EPS = 1e-5

MIX_TILE = 512
CONV_ROWS = 32
SUBLANES = 8
SHIFT_ROWS = 136
EXPERT_BLOCK = 512
DISPATCH_TILE = 1024
COMBINE_TILE = 512
ISSUE_UNROLL = 8
PAD_BITS = tuple(1 << b for b in reversed(range(EXPERT_BLOCK.bit_length() - 1)))
VMEM_LIMIT = 56 * 1024 * 1024

F32 = jnp.float32
BF16 = jnp.bfloat16


def _rms(x, g):
    return x * lax.rsqrt(jnp.mean(x * x, axis=-1, keepdims=True) + EPS) * g


def _ln(x, g, b):
    mu = jnp.mean(x, axis=-1, keepdims=True)
    xc = x - mu
    var = jnp.mean(xc * xc, axis=-1, keepdims=True)
    return xc * lax.rsqrt(var + EPS) * g + b


def _gelu(x):
    return 0.5 * x * (1.0 + lax.erf(x * (2.0 ** -0.5)))


LANES = 128
TOKEN_LINES = D_MODEL // LANES


GATHER_PITCH = TOKEN_LINES + 1


def _load_token_tiles(ref, rows, lead=(), pitch=TOKEN_LINES):
    chunks = [ref[lead + (pl.ds(c, rows, stride=pitch), slice(None))] for c in range(TOKEN_LINES)]
    return jnp.concatenate(chunks, axis=1)


def _store_token_tiles(ref, value, rows):
    for c in range(TOKEN_LINES):
        ref[pl.ds(c, rows, stride=TOKEN_LINES), :] = value[:, c * LANES:(c + 1) * LANES]


def _mix_kernel(x_ref, g1_ref, win_ref, bin_ref, cw_ref, cb_ref, clg_ref, clb_ref,
                slg_ref, slb_ref, sgw_ref, sgb_ref, ggc_ref, ggs_ref, wout_ref, bout_ref,
                g2_ref, wr_ref, br_ref,
                x1_ref, h2_ref, idx_ref, gate_ref, rank_ref, cnt_ref,
                ubuf, ushift, ycat, cnt_s):
    T = MIX_TILE
    b = pl.program_id(0)
    s = pl.program_id(1)

    @pl.when(s == 0)
    def _():
        ubuf[0:CONV_HIST, :] = jnp.zeros((CONV_HIST, D_CONV), F32)

    @pl.when((b == 0) & (s == 0))
    def _():
        cnt_s[...] = jnp.zeros_like(cnt_s)

    x = x_ref[...]
    h = _rms(x, g1_ref[...])
    p = jnp.dot(h.astype(BF16), win_ref[...], preferred_element_type=F32) + bin_ref[...]

    u = p[:, 0:D_CONV] * jax.nn.sigmoid(p[:, D_CONV:2 * D_CONV])
    ubuf[CONV_HIST:CONV_HIST + T, :] = u

    n_rows = T + CONV_HIST
    for r in range(1, SUBLANES):
        ushift[r - 1] = pltpu.roll(ubuf[...], n_rows - r, axis=0)

    for c in range(T // CONV_ROWS):
        r0 = c * CONV_ROWS
        acc = jnp.broadcast_to(cb_ref[...], (CONV_ROWS, D_CONV))
        for k in range(CONV_WIDTH):
            off = CONV_HIST - (CONV_WIDTH - 1) + k
            q, r = divmod(off, SUBLANES)
            a0 = r0 + q * SUBLANES
            if r == 0:
                tap = ubuf[a0:a0 + CONV_ROWS, :]
            else:
                tap = ushift[r - 1, a0:a0 + CONV_ROWS, :]
            w8 = cw_ref[k * SUBLANES:(k + 1) * SUBLANES, :]
            acc = acc + jnp.concatenate([w8] * (CONV_ROWS // SUBLANES), axis=0) * tap
        y = _ln(acc, clg_ref[...], clb_ref[...])
        y = y * jax.nn.sigmoid(y)
        y = _rms(y, ggc_ref[...])
        ycat[r0:r0 + CONV_ROWS, 0:D_CONV] = y.astype(BF16)
    ubuf[0:CONV_HIST, :] = ubuf[T:T + CONV_HIST, :]

    uu = _gelu(p[:, 2 * D_CONV:2 * D_CONV + D_GMLP])
    vv = _ln(_gelu(p[:, 2 * D_CONV + D_GMLP:]), slg_ref[...], slb_ref[...])
    t_out = lax.broadcasted_iota(jnp.int32, (GMLP_BLOCK, 2 * GMLP_BLOCK), 0)
    s_in = lax.broadcasted_iota(jnp.int32, (GMLP_BLOCK, 2 * GMLP_BLOCK), 1) % GMLP_BLOCK
    wmask = (s_in // CHUNK) <= (t_out // CHUNK)
    lane = lax.broadcasted_iota(jnp.int32, (GMLP_BLOCK, 2 * GMLP_HEAD_DIM), 1)
    lo = lane < GMLP_HEAD_DIM
    n_pair = GMLP_HEADS // 2
    wcat = [jnp.where(wmask, sgw_ref[j], 0.0).astype(BF16) for j in range(n_pair)]
    for blk in range(T // GMLP_BLOCK):
        rows = slice(blk * GMLP_BLOCK, (blk + 1) * GMLP_BLOCK)
        outs = []
        for j in range(n_pair):
            cols = slice(j * 128, (j + 1) * 128)
            vp = vv[rows, cols]
            vstack = jnp.concatenate(
                [jnp.where(lo, vp, 0.0), jnp.where(lo, 0.0, vp)], axis=0).astype(BF16)
            mixed = jnp.dot(wcat[j], vstack, preferred_element_type=F32) + sgb_ref[:, cols]
            outs.append(uu[rows, cols] * mixed)
        ysg = jnp.concatenate(outs, axis=1)
        ycat[rows, D_CONV:] = _rms(ysg, ggs_ref[...]).astype(BF16)

    o = jnp.dot(ycat[...], wout_ref[...], preferred_element_type=F32) + bout_ref[...]
    x1 = x + o
    x1_ref[...] = x1
    h2 = _rms(x1, g2_ref[...])
    _store_token_tiles(h2_ref, h2, T)

    logits = lax.dot_general(wr_ref[...].astype(BF16), h2.astype(BF16),
                             (((1,), (1,)), ((), ())),
                             preferred_element_type=F32) + br_ref[...]
    e_iota = lax.broadcasted_iota(jnp.int32, (N_EXPERTS, T), 0)
    work = logits
    vals, idxs, sels = [], [], []
    for _k in range(TOP_K):
        m = jnp.max(work, axis=0, keepdims=True)
        am = jnp.min(jnp.where(work == m, e_iota, N_EXPERTS), axis=0, keepdims=True)
        sel = e_iota == am
        vals.append(m)
        idxs.append(am)
        sels.append(sel)
        work = jnp.where(sel, -jnp.inf, work)
    exps = [jnp.exp(v - vals[0]) for v in vals]
    denom = exps[0] + exps[1] + exps[2] + exps[3]
    gate_ref[...] = jnp.concatenate([e / denom for e in exps], axis=0)
    idx_ref[...] = jnp.concatenate(idxs, axis=0)

    member = jnp.zeros((N_EXPERTS, T), F32)
    for sel in sels:
        member = member + jnp.where(sel, 1.0, 0.0)
    r_i = lax.broadcasted_iota(jnp.int32, (T, T), 0)
    c_i = lax.broadcasted_iota(jnp.int32, (T, T), 1)
    upper = jnp.where(r_i < c_i, 1.0, 0.0).astype(BF16)
    before = jnp.dot(member.astype(BF16), upper, preferred_element_type=F32)
    before = before + cnt_s[:, 0:1]
    ranks = [jnp.sum(jnp.where(sel, before, 0.0), axis=0, keepdims=True) for sel in sels]
    rank_ref[...] = jnp.concatenate(ranks, axis=0).astype(jnp.int32)
    cnt_new = cnt_s[...] + jnp.sum(member, axis=1, keepdims=True)
    cnt_s[...] = cnt_new
    cnt_ref[...] = cnt_new.astype(jnp.int32)


def _dispatch_kernel(dest_ref, padlo_ref, padn_ref, tot_ref, h2_ref, xs_ref, zbuf, sem, zsem, tsem):
    i = pl.program_id(0)
    blk_lines = EXPERT_BLOCK * TOKEN_LINES
    n_blocks = xs_ref.shape[0] // blk_lines

    def pad_pieces(e, op):
        n = padn_ref[e]
        line = padlo_ref[e]
        for bit in PAD_BITS:
            hit = (n & bit) != 0

            @pl.when(hit)
            def _(line=line, bit=bit):
                op(pltpu.make_async_copy(
                    zbuf.at[pl.ds(0, bit * TOKEN_LINES)],
                    xs_ref.at[pl.ds(pl.multiple_of(line, TOKEN_LINES), bit * TOKEN_LINES)], zsem))

            line = line + jnp.where(hit, bit * TOKEN_LINES, 0)

    def for_each_expert(op):
        def body(e, carry):
            pad_pieces(e, op)
            return carry
        lax.fori_loop(0, N_EXPERTS, body, 0)

    @pl.when(i == 0)
    def _():
        zbuf[...] = jnp.zeros_like(zbuf)
        for_each_expert(lambda c: c.start())

    tail_blk = tot_ref[0] + i

    def tail_copy():
        line = pl.multiple_of(tail_blk * blk_lines, blk_lines)
        return pltpu.make_async_copy(zbuf, xs_ref.at[pl.ds(line, blk_lines)], tsem)

    @pl.when(tail_blk < n_blocks)
    def _():
        tail_copy().start()

    def start(g, carry):
        for j in range(ISSUE_UNROLL):
            t = g * ISSUE_UNROLL + j
            src = h2_ref.at[pl.ds(pl.multiple_of(t * TOKEN_LINES, TOKEN_LINES), TOKEN_LINES)]
            for k in range(TOP_K):
                line = pl.multiple_of(dest_ref[k, t], TOKEN_LINES)
                pltpu.make_async_copy(src, xs_ref.at[pl.ds(line, TOKEN_LINES)], sem).start(priority=k % 2)
        return carry

    lax.fori_loop(0, DISPATCH_TILE // ISSUE_UNROLL, start, 0)
    for k in range(TOP_K):
        pltpu.make_async_copy(h2_ref, xs_ref.at[pl.ds(0, DISPATCH_TILE * TOKEN_LINES)], sem).wait()

    @pl.when(i == 0)
    def _():
        for_each_expert(lambda c: c.wait())

    @pl.when(tail_blk < n_blocks)
    def _():
        tail_copy().wait()


def _expert_kernel(be_ref, tot_ref, xs_ref, w1_ref, b1_ref, w2_ref, b2_ref, y_ref):
    del be_ref
    i = pl.program_id(0)

    @pl.when(i < tot_ref[0])
    def _():
        xb = _load_token_tiles(xs_ref, EXPERT_BLOCK).astype(BF16)
        hcat = lax.dot_general(xb, w1_ref[...], (((1,), (0,)), ((), ())),
                               preferred_element_type=F32) + b1_ref[...]
        x_glu = jnp.minimum(hcat[:, :D_FF], SWIGLU_LIMIT)
        x_lin = jnp.clip(hcat[:, D_FF:], -SWIGLU_LIMIT, SWIGLU_LIMIT)
        act = x_glu * jax.nn.sigmoid(SWIGLU_ALPHA * x_glu) * (x_lin + 1.0)
        y = lax.dot_general(act.astype(BF16), w2_ref[...], (((1,), (0,)), ((), ())),
                            preferred_element_type=F32) + b2_ref[...]
        _store_token_tiles(y_ref, y, EXPERT_BLOCK)

    @pl.when(i >= tot_ref[0])
    def _():
        y_ref[...] = jnp.zeros_like(y_ref)


def _combine_kernel(dest_ref, dnext_ref, x1_ref, gate_ref, fg_ref, y_ref, out_ref, buf, sem):
    TC = COMBINE_TILE
    i = pl.program_id(0)
    cur = i % 2
    nxt = (i + 1) % 2

    def issue(d_ref, slot):
        def start(g, carry):
            for j in range(ISSUE_UNROLL):
                t = g * ISSUE_UNROLL + j
                row = t * GATHER_PITCH
                for k in range(TOP_K):
                    line = pl.multiple_of(d_ref[k, t], TOKEN_LINES)
                    pltpu.make_async_copy(
                        y_ref.at[pl.ds(line, TOKEN_LINES)],
                        buf.at[slot, k, pl.ds(row, TOKEN_LINES)],
                        sem.at[slot]).start(priority=k % 2)
            return carry

        lax.fori_loop(0, TC // ISSUE_UNROLL, start, 0)

    @pl.when(i == 0)
    def _():
        issue(dest_ref, cur)

    @pl.when(i + 1 < pl.num_programs(0))
    def _():
        issue(dnext_ref, nxt)

    for k in range(TOP_K):
        pltpu.make_async_copy(y_ref.at[pl.ds(0, TC * TOKEN_LINES)],
                              buf.at[cur, k, pl.ds(0, TC * TOKEN_LINES)], sem.at[cur]).wait()

    gpad = jnp.concatenate([gate_ref[...], jnp.zeros((128 - TOP_K, TC), F32)], axis=0)
    gt = gpad.T
    acc = x1_ref[...]
    for k in range(TOP_K):
        acc = acc + gt[:, k:k + 1] * _load_token_tiles(buf, TC, lead=(cur, k), pitch=GATHER_PITCH)
    out_ref[...] = _rms(acc, fg_ref[...])


def _full(shape):
    n = len(shape)
    return pl.BlockSpec(shape, lambda *_: (0,) * n)


def kernel(x, norm1_g, w_in, b_in, conv_w, conv_b, conv_ln_g, conv_ln_b, sg_ln_g, sg_ln_b,
           sg_w, sg_b, grp_g_conv, grp_g_sg, w_out, b_out, norm2_g, w_router, b_router,
           w_exp1, b_exp1, w_exp2, b_exp2, final_g):
    B, S, D = x.shape
    assert D == D_MODEL and S % MIX_TILE == 0 and norm1_g.shape[0] == 1
    N = B * S
    T = MIX_TILE
    n_s = S // T
    l = 0

    row = lambda v: v.reshape(1, -1).astype(F32)
    cw = jnp.repeat(conv_w[l].astype(F32), SUBLANES, axis=0)
    sgw = sg_w[l].reshape(GMLP_HEADS // 2, 2, GMLP_BLOCK, GMLP_BLOCK)
    sgw = sgw.transpose(0, 2, 1, 3).reshape(GMLP_HEADS // 2, GMLP_BLOCK, 2 * GMLP_BLOCK)
    sgb = jnp.repeat(sg_b[l].T, GMLP_HEAD_DIM, axis=1)
    wr_t = w_router[l].T
    br = b_router[l].reshape(N_EXPERTS, 1)

    mix_in = [
        x, row(norm1_g[l]), w_in[l].astype(BF16), row(b_in[l]), cw, row(conv_b[l]),
        row(conv_ln_g[l]), row(conv_ln_b[l]), row(sg_ln_g[l]), row(sg_ln_b[l]), sgw, sgb,
        row(grp_g_conv[l]), row(grp_g_sg[l]), w_out[l].astype(BF16), row(b_out[l]),
        row(norm2_g[l]), wr_t, br,
    ]
    mix_specs = [pl.BlockSpec((None, T, D), lambda b, s: (b, s, 0))]
    mix_specs += [_full(a.shape) for a in mix_in[1:]]
    tok_spec = pl.BlockSpec((TOP_K, T), lambda b, s: (0, b * n_s + s))
    x1, h2, idx, gate, rank, cnt = pl.pallas_call(
        _mix_kernel,
        grid=(B, n_s),
        in_specs=mix_specs,
        out_specs=[
            pl.BlockSpec((None, T, D), lambda b, s: (b, s, 0)),
            pl.BlockSpec((T * TOKEN_LINES, LANES), lambda b, s: (b * n_s + s, 0)),
            tok_spec, tok_spec, tok_spec,
            pl.BlockSpec((N_EXPERTS, 128), lambda b, s: (0, 0)),
        ],
        out_shape=[
            jax.ShapeDtypeStruct((B, S, D), F32),
            jax.ShapeDtypeStruct((N * TOKEN_LINES, LANES), F32),
            jax.ShapeDtypeStruct((TOP_K, N), jnp.int32),
            jax.ShapeDtypeStruct((TOP_K, N), F32),
            jax.ShapeDtypeStruct((TOP_K, N), jnp.int32),
            jax.ShapeDtypeStruct((N_EXPERTS, 128), jnp.int32),
        ],
        scratch_shapes=[
            pltpu.VMEM((T + CONV_HIST, D_CONV), F32),
            pltpu.VMEM((SUBLANES - 1, T + CONV_HIST, D_CONV), F32),
            pltpu.VMEM((T, D_MODEL), BF16),
            pltpu.VMEM((N_EXPERTS, 128), F32),
        ],
        compiler_params=pltpu.CompilerParams(
            dimension_semantics=("arbitrary", "arbitrary"), vmem_limit_bytes=VMEM_LIMIT),
        name="mix",
    )(*mix_in)

    blk = EXPERT_BLOCK
    n_blocks = (N * TOP_K) // blk + N_EXPERTS
    n_slots = n_blocks * blk
    counts = cnt[:, 0]
    nblk_e = (counts + blk - 1) // blk
    block_end = jnp.cumsum(nblk_e)
    pad_start = (block_end - nblk_e) * blk
    dest = rank
    for e in range(N_EXPERTS):
        dest = dest + jnp.where(idx == e, pad_start[e], 0)
    dest = dest * TOKEN_LINES
    blk_ids = jnp.arange(n_blocks, dtype=jnp.int32)
    block_e = jnp.minimum(
        jnp.sum((blk_ids[:, None] >= block_end[None, :]).astype(jnp.int32), axis=1),
        N_EXPERTS - 1).astype(jnp.int32)
    total = block_end[-1:].astype(jnp.int32)
    pad_lo = ((pad_start + counts) * TOKEN_LINES).astype(jnp.int32)
    pad_n = (nblk_e * blk - counts).astype(jnp.int32)
    assert N // DISPATCH_TILE >= N_EXPERTS

    xs = pl.pallas_call(
        _dispatch_kernel,
        grid=(N // DISPATCH_TILE,),
        in_specs=[
            pl.BlockSpec((TOP_K, DISPATCH_TILE), lambda i: (0, i), memory_space=pltpu.SMEM),
            pl.BlockSpec(memory_space=pltpu.SMEM),
            pl.BlockSpec(memory_space=pltpu.SMEM),
            pl.BlockSpec(memory_space=pltpu.SMEM),
            pl.BlockSpec((DISPATCH_TILE * TOKEN_LINES, LANES), lambda i: (i, 0)),
        ],
        out_specs=pl.BlockSpec(memory_space=pl.ANY),
        out_shape=jax.ShapeDtypeStruct((n_slots * TOKEN_LINES, LANES), F32),
        scratch_shapes=[pltpu.VMEM((blk * TOKEN_LINES, LANES), F32),
                        pltpu.SemaphoreType.DMA(()), pltpu.SemaphoreType.DMA(()),
                        pltpu.SemaphoreType.DMA(())],
        compiler_params=pltpu.CompilerParams(
            dimension_semantics=("arbitrary",), vmem_limit_bytes=VMEM_LIMIT),
        name="dispatch",
    )(dest, pad_lo, pad_n, total, h2)

    def blk_map(i, be, tot):
        return (jnp.minimum(i, tot[0] - 1), 0)

    def exp_map(i, be, tot):
        return (be[jnp.minimum(i, tot[0] - 1)], 0, 0)

    y = pl.pallas_call(
        _expert_kernel,
        grid_spec=pltpu.PrefetchScalarGridSpec(
            num_scalar_prefetch=2,
            grid=(n_blocks,),
            in_specs=[
                pl.BlockSpec((blk * TOKEN_LINES, LANES), blk_map),
                pl.BlockSpec((None, D, 2 * D_FF), exp_map),
                pl.BlockSpec((None, 1, 2 * D_FF), exp_map),
                pl.BlockSpec((None, D_FF, D), exp_map),
                pl.BlockSpec((None, 1, D), exp_map),
            ],
            out_specs=pl.BlockSpec((blk * TOKEN_LINES, LANES), lambda i, be, tot: (i, 0)),
        ),
        out_shape=jax.ShapeDtypeStruct((n_slots * TOKEN_LINES, LANES), F32),
        compiler_params=pltpu.CompilerParams(
            dimension_semantics=("arbitrary",), vmem_limit_bytes=VMEM_LIMIT),
        name="experts",
    )(block_e, total, xs, w_exp1[l], b_exp1[l].reshape(N_EXPERTS, 1, 2 * D_FF),
      w_exp2[l], b_exp2[l].reshape(N_EXPERTS, 1, D))

    TC = COMBINE_TILE
    out = pl.pallas_call(
        _combine_kernel,
        grid=(N // TC,),
        in_specs=[
            pl.BlockSpec((TOP_K, TC), lambda i: (0, i), memory_space=pltpu.SMEM),
            pl.BlockSpec((TOP_K, TC), lambda i: (0, jnp.minimum(i + 1, N // TC - 1)),
                         memory_space=pltpu.SMEM),
            pl.BlockSpec((TC, D), lambda i: (i, 0)),
            pl.BlockSpec((TOP_K, TC), lambda i: (0, i)),
            _full((1, D)),
            pl.BlockSpec(memory_space=pl.ANY),
        ],
        out_specs=pl.BlockSpec((TC, D), lambda i: (i, 0)),
        out_shape=jax.ShapeDtypeStruct((N, D), F32),
        scratch_shapes=[pltpu.VMEM((2, TOP_K, TC * GATHER_PITCH, LANES), F32),
                        pltpu.SemaphoreType.DMA((2,))],
        compiler_params=pltpu.CompilerParams(
            dimension_semantics=("arbitrary",), vmem_limit_bytes=VMEM_LIMIT),
        name="combine",
    )(dest, dest, x1.reshape(N, D), gate, row(final_g), y)
    return out.reshape(B, S, D)
```

```python
import functools

import jax
import jax.numpy as jnp
from jax import lax
from jax.experimental import pallas as pl
from jax.experimental.pallas import tpu as pltpu

D_MODEL = 1024
D_CONV = 512
D_GMLP = 512
CONV_WIDTH = 31
CONV_HIST = 32
GMLP_HEADS = 8
GMLP_HEAD_DIM = 64
GMLP_BLOCK = 128
CHUNK = 64
N_EXPERTS = 32
TOP_K = 4
D_FF = 1024
SWIGLU_ALPHA = 1.702
SWIGLU_LIMIT = 7.0
EPS = 1e-5

MIX_TILE = 512
CONV_ROWS = 32
SUBLANES = 8
SHIFT_ROWS = 136
EXPERT_BLOCK = 512
DISPATCH_TILE = 1024
COMBINE_TILE = 512
ISSUE_UNROLL = 8
PAD_BITS = tuple(1 << b for b in reversed(range(EXPERT_BLOCK.bit_length() - 1)))
VMEM_LIMIT = 56 * 1024 * 1024

F32 = jnp.float32
BF16 = jnp.bfloat16


def _rms(x, g):
    return x * lax.rsqrt(jnp.mean(x * x, axis=-1, keepdims=True) + EPS) * g


def _ln(x, g, b):
    mu = jnp.mean(x, axis=-1, keepdims=True)
    xc = x - mu
    var = jnp.mean(xc * xc, axis=-1, keepdims=True)
    return xc * lax.rsqrt(var + EPS) * g + b


def _gelu(x):
    return 0.5 * x * (1.0 + lax.erf(x * (2.0 ** -0.5)))


LANES = 128
TOKEN_LINES = D_MODEL // LANES


GATHER_PITCH = TOKEN_LINES + 1


def _load_token_tiles(ref, rows, lead=(), pitch=TOKEN_LINES):
    chunks = [ref[lead + (pl.ds(c, rows, stride=pitch), slice(None))] for c in range(TOKEN_LINES)]
    return jnp.concatenate(chunks, axis=1)


def _store_token_tiles(ref, value, rows):
    for c in range(TOKEN_LINES):
        ref[pl.ds(c, rows, stride=TOKEN_LINES), :] = value[:, c * LANES:(c + 1) * LANES]


def _mix_kernel(x_ref, g1_ref, win_ref, bin_ref, cw_ref, cb_ref, clg_ref, clb_ref,
                slg_ref, slb_ref, sgw_ref, sgb_ref, ggc_ref, ggs_ref, wout_ref, bout_ref,
                g2_ref, wr_ref, br_ref,
                x1_ref, h2_ref, idx_ref, gate_ref, rank_ref, cnt_ref,
                ubuf, ushift, ycat, cnt_s):
    T = MIX_TILE
    b = pl.program_id(0)
    s = pl.program_id(1)

    @pl.when(s == 0)
    def _():
        ubuf[0:CONV_HIST, :] = jnp.zeros((CONV_HIST, D_CONV), F32)

    @pl.when((b == 0) & (s == 0))
    def _():
        cnt_s[...] = jnp.zeros_like(cnt_s)

    x = x_ref[...]
    h = _rms(x, g1_ref[...])
    p = jnp.dot(h.astype(BF16), win_ref[...], preferred_element_type=F32) + bin_ref[...]

    u = p[:, 0:D_CONV] * jax.nn.sigmoid(p[:, D_CONV:2 * D_CONV])
    ubuf[CONV_HIST:CONV_HIST + T, :] = u

    n_rows = T + CONV_HIST
    for r in range(1, SUBLANES):
        ushift[r - 1] = pltpu.roll(ubuf[...], n_rows - r, axis=0)

    for c in range(T // CONV_ROWS):
        r0 = c * CONV_ROWS
        acc = jnp.broadcast_to(cb_ref[...], (CONV_ROWS, D_CONV))
        for k in range(CONV_WIDTH):
            off = CONV_HIST - (CONV_WIDTH - 1) + k
            q, r = divmod(off, SUBLANES)
            a0 = r0 + q * SUBLANES
            if r == 0:
                tap = ubuf[a0:a0 + CONV_ROWS, :]
            else:
                tap = ushift[r - 1, a0:a0 + CONV_ROWS, :]
            w8 = cw_ref[k * SUBLANES:(k + 1) * SUBLANES, :]
            acc = acc + jnp.concatenate([w8] * (CONV_ROWS // SUBLANES), axis=0) * tap
        y = _ln(acc, clg_ref[...], clb_ref[...])
        y = y * jax.nn.sigmoid(y)
        y = _rms(y, ggc_ref[...])
        ycat[r0:r0 + CONV_ROWS, 0:D_CONV] = y.astype(BF16)
    ubuf[0:CONV_HIST, :] = ubuf[T:T + CONV_HIST, :]

    uu = _gelu(p[:, 2 * D_CONV:2 * D_CONV + D_GMLP])
    vv = _ln(_gelu(p[:, 2 * D_CONV + D_GMLP:]), slg_ref[...], slb_ref[...])
    t_out = lax.broadcasted_iota(jnp.int32, (GMLP_BLOCK, 2 * GMLP_BLOCK), 0)
    s_in = lax.broadcasted_iota(jnp.int32, (GMLP_BLOCK, 2 * GMLP_BLOCK), 1) % GMLP_BLOCK
    wmask = (s_in // CHUNK) <= (t_out // CHUNK)
    lane = lax.broadcasted_iota(jnp.int32, (GMLP_BLOCK, 2 * GMLP_HEAD_DIM), 1)
    lo = lane < GMLP_HEAD_DIM
    n_pair = GMLP_HEADS // 2
    wcat = [jnp.where(wmask, sgw_ref[j], 0.0).astype(BF16) for j in range(n_pair)]
    for blk in range(T // GMLP_BLOCK):
        rows = slice(blk * GMLP_BLOCK, (blk + 1) * GMLP_BLOCK)
        outs = []
        for j in range(n_pair):
            cols = slice(j * 128, (j + 1) * 128)
            vp = vv[rows, cols]
            vstack = jnp.concatenate(
                [jnp.where(lo, vp, 0.0), jnp.where(lo, 0.0, vp)], axis=0).astype(BF16)
            mixed = jnp.dot(wcat[j], vstack, preferred_element_type=F32) + sgb_ref[:, cols]
            outs.append(uu[rows, cols] * mixed)
        ysg = jnp.concatenate(outs, axis=1)
        ycat[rows, D_CONV:] = _rms(ysg, ggs_ref[...]).astype(BF16)

    o = jnp.dot(ycat[...], wout_ref[...], preferred_element_type=F32) + bout_ref[...]
    x1 = x + o
    x1_ref[...] = x1
    h2 = _rms(x1, g2_ref[...])
    _store_token_tiles(h2_ref, h2, T)

    logits = lax.dot_general(wr_ref[...].astype(BF16), h2.astype(BF16),
                             (((1,), (1,)), ((), ())),
                             preferred_element_type=F32) + br_ref[...]
    e_iota = lax.broadcasted_iota(jnp.int32, (N_EXPERTS, T), 0)
    work = logits
    vals, idxs, sels = [], [], []
    for _k in range(TOP_K):
        m = jnp.max(work, axis=0, keepdims=True)
        am = jnp.min(jnp.where(work == m, e_iota, N_EXPERTS), axis=0, keepdims=True)
        sel = e_iota == am
        vals.append(m)
        idxs.append(am)
        sels.append(sel)
        work = jnp.where(sel, -jnp.inf, work)
    exps = [jnp.exp(v - vals[0]) for v in vals]
    denom = exps[0] + exps[1] + exps[2] + exps[3]
    gate_ref[...] = jnp.concatenate([e / denom for e in exps], axis=0)
    idx_ref[...] = jnp.concatenate(idxs, axis=0)

    member = jnp.zeros((N_EXPERTS, T), F32)
    for sel in sels:
        member = member + jnp.where(sel, 1.0, 0.0)
    r_i = lax.broadcasted_iota(jnp.int32, (T, T), 0)
    c_i = lax.broadcasted_iota(jnp.int32, (T, T), 1)
    upper = jnp.where(r_i < c_i, 1.0, 0.0).astype(BF16)
    before = jnp.dot(member.astype(BF16), upper, preferred_element_type=F32)
    before = before + cnt_s[:, 0:1]
    ranks = [jnp.sum(jnp.where(sel, before, 0.0), axis=0, keepdims=True) for sel in sels]
    rank_ref[...] = jnp.concatenate(ranks, axis=0).astype(jnp.int32)
    cnt_new = cnt_s[...] + jnp.sum(member, axis=1, keepdims=True)
    cnt_s[...] = cnt_new
    cnt_ref[...] = cnt_new.astype(jnp.int32)


def _dispatch_kernel(dest_ref, padlo_ref, padn_ref, tot_ref, h2_ref, xs_ref, zbuf, sem, zsem, tsem):
    i = pl.program_id(0)
    blk_lines = EXPERT_BLOCK * TOKEN_LINES
    n_blocks = xs_ref.shape[0] // blk_lines

    def pad_pieces(e, op):
        n = padn_ref[e]
        line = padlo_ref[e]
        for bit in PAD_BITS:
            hit = (n & bit) != 0

            @pl.when(hit)
            def _(line=line, bit=bit):
                op(pltpu.make_async_copy(
                    zbuf.at[pl.ds(0, bit * TOKEN_LINES)],
                    xs_ref.at[pl.ds(pl.multiple_of(line, TOKEN_LINES), bit * TOKEN_LINES)], zsem))

            line = line + jnp.where(hit, bit * TOKEN_LINES, 0)

    def for_each_expert(op):
        def body(e, carry):
            pad_pieces(e, op)
            return carry
        lax.fori_loop(0, N_EXPERTS, body, 0)

    @pl.when(i == 0)
    def _():
        zbuf[...] = jnp.zeros_like(zbuf)
        for_each_expert(lambda c: c.start())

    tail_blk = tot_ref[0] + i

    def tail_copy():
        line = pl.multiple_of(tail_blk * blk_lines, blk_lines)
        return pltpu.make_async_copy(zbuf, xs_ref.at[pl.ds(line, blk_lines)], tsem)

    @pl.when(tail_blk < n_blocks)
    def _():
        tail_copy().start()

    def start(g, carry):
        for j in range(ISSUE_UNROLL):
            t = g * ISSUE_UNROLL + j
            src = h2_ref.at[pl.ds(pl.multiple_of(t * TOKEN_LINES, TOKEN_LINES), TOKEN_LINES)]
            for k in range(TOP_K):
                line = pl.multiple_of(dest_ref[k, t], TOKEN_LINES)
                pltpu.make_async_copy(src, xs_ref.at[pl.ds(line, TOKEN_LINES)], sem).start(priority=k % 2)
        return carry

    lax.fori_loop(0, DISPATCH_TILE // ISSUE_UNROLL, start, 0)
    for k in range(TOP_K):
        pltpu.make_async_copy(h2_ref, xs_ref.at[pl.ds(0, DISPATCH_TILE * TOKEN_LINES)], sem).wait()

    @pl.when(i == 0)
    def _():
        for_each_expert(lambda c: c.wait())

    @pl.when(tail_blk < n_blocks)
    def _():
        tail_copy().wait()


def _expert_kernel(be_ref, tot_ref, nxt_ref, slot_ref, xs_ref, w1_hbm, b1_ref, w2_hbm, b2_ref, y_ref,
                   w1buf, w2buf, wsem1, wsem2):
    i = pl.program_id(0)

    def fetch(e, s):
        return (pltpu.make_async_copy(w1_hbm.at[e], w1buf.at[s], wsem1.at[s]),
                pltpu.make_async_copy(w2_hbm.at[e], w2buf.at[s], wsem2.at[s]))

    @pl.when(i < tot_ref[0])
    def _():
        e = be_ref[i]
        s = slot_ref[i]

        @pl.when(i == 0)
        def _():
            for cp in fetch(e, s):
                cp.start()

        @pl.when((i == 0) | (e != be_ref[jnp.maximum(i - 1, 0)]))
        def _():
            for cp in fetch(e, s):
                cp.wait()

            @pl.when(nxt_ref[i] >= 0)
            def _():
                for cp in fetch(nxt_ref[i], 1 - s):
                    cp.start()

        xb = _load_token_tiles(xs_ref, EXPERT_BLOCK).astype(BF16)
        hcat = lax.dot_general(xb, w1buf[s], (((1,), (0,)), ((), ())),
                               preferred_element_type=F32) + b1_ref[...]
        x_glu = jnp.minimum(hcat[:, :D_FF], SWIGLU_LIMIT)
        x_lin = jnp.clip(hcat[:, D_FF:], -SWIGLU_LIMIT, SWIGLU_LIMIT)
        act = x_glu * jax.nn.sigmoid(SWIGLU_ALPHA * x_glu) * (x_lin + 1.0)
        y = lax.dot_general(act.astype(BF16), w2buf[s], (((1,), (0,)), ((), ())),
                            preferred_element_type=F32) + b2_ref[...]
        _store_token_tiles(y_ref, y, EXPERT_BLOCK)

    @pl.when(i >= tot_ref[0])
    def _():
        y_ref[...] = jnp.zeros_like(y_ref)


def _combine_kernel(dest_ref, dnext_ref, x1_ref, gate_ref, fg_ref, y_ref, out_ref, buf, sem):
    TC = COMBINE_TILE
    i = pl.program_id(0)
    cur = i % 2
    nxt = (i + 1) % 2

    def issue(d_ref, slot):
        def start(g, carry):
            for j in range(ISSUE_UNROLL):
                t = g * ISSUE_UNROLL + j
                row = t * GATHER_PITCH
                for k in range(TOP_K):
                    line = pl.multiple_of(d_ref[k, t], TOKEN_LINES)
                    pltpu.make_async_copy(
                        y_ref.at[pl.ds(line, TOKEN_LINES)],
                        buf.at[slot, k, pl.ds(row, TOKEN_LINES)],
                        sem.at[slot]).start(priority=k % 2)
            return carry

        lax.fori_loop(0, TC // ISSUE_UNROLL, start, 0)

    @pl.when(i == 0)
    def _():
        issue(dest_ref, cur)

    @pl.when(i + 1 < pl.num_programs(0))
    def _():
        issue(dnext_ref, nxt)

    for k in range(TOP_K):
        pltpu.make_async_copy(y_ref.at[pl.ds(0, TC * TOKEN_LINES)],
                              buf.at[cur, k, pl.ds(0, TC * TOKEN_LINES)], sem.at[cur]).wait()

    gpad = jnp.concatenate([gate_ref[...], jnp.zeros((128 - TOP_K, TC), F32)], axis=0)
    gt = gpad.T
    acc = x1_ref[...]
    for k in range(TOP_K):
        acc = acc + gt[:, k:k + 1] * _load_token_tiles(buf, TC, lead=(cur, k), pitch=GATHER_PITCH)
    out_ref[...] = _rms(acc, fg_ref[...])


def _full(shape):
    n = len(shape)
    return pl.BlockSpec(shape, lambda *_: (0,) * n)


def kernel(x, norm1_g, w_in, b_in, conv_w, conv_b, conv_ln_g, conv_ln_b, sg_ln_g, sg_ln_b,
           sg_w, sg_b, grp_g_conv, grp_g_sg, w_out, b_out, norm2_g, w_router, b_router,
           w_exp1, b_exp1, w_exp2, b_exp2, final_g):
    B, S, D = x.shape
    assert D == D_MODEL and S % MIX_TILE == 0 and norm1_g.shape[0] == 1
    N = B * S
    T = MIX_TILE
    n_s = S // T
    l = 0

    row = lambda v: v.reshape(1, -1).astype(F32)
    cw = jnp.repeat(conv_w[l].astype(F32), SUBLANES, axis=0)
    sgw = sg_w[l].reshape(GMLP_HEADS // 2, 2, GMLP_BLOCK, GMLP_BLOCK)
    sgw = sgw.transpose(0, 2, 1, 3).reshape(GMLP_HEADS // 2, GMLP_BLOCK, 2 * GMLP_BLOCK)
    sgb = jnp.repeat(sg_b[l].T, GMLP_HEAD_DIM, axis=1)
    wr_t = w_router[l].T
    br = b_router[l].reshape(N_EXPERTS, 1)

    mix_in = [
        x, row(norm1_g[l]), w_in[l].astype(BF16), row(b_in[l]), cw, row(conv_b[l]),
        row(conv_ln_g[l]), row(conv_ln_b[l]), row(sg_ln_g[l]), row(sg_ln_b[l]), sgw, sgb,
        row(grp_g_conv[l]), row(grp_g_sg[l]), w_out[l].astype(BF16), row(b_out[l]),
        row(norm2_g[l]), wr_t, br,
    ]
    mix_specs = [pl.BlockSpec((None, T, D), lambda b, s: (b, s, 0))]
    mix_specs += [_full(a.shape) for a in mix_in[1:]]
    tok_spec = pl.BlockSpec((TOP_K, T), lambda b, s: (0, b * n_s + s))
    x1, h2, idx, gate, rank, cnt = pl.pallas_call(
        _mix_kernel,
        grid=(B, n_s),
        in_specs=mix_specs,
        out_specs=[
            pl.BlockSpec((None, T, D), lambda b, s: (b, s, 0)),
            pl.BlockSpec((T * TOKEN_LINES, LANES), lambda b, s: (b * n_s + s, 0)),
            tok_spec, tok_spec, tok_spec,
            pl.BlockSpec((N_EXPERTS, 128), lambda b, s: (0, 0)),
        ],
        out_shape=[
            jax.ShapeDtypeStruct((B, S, D), F32),
            jax.ShapeDtypeStruct((N * TOKEN_LINES, LANES), F32),
            jax.ShapeDtypeStruct((TOP_K, N), jnp.int32),
            jax.ShapeDtypeStruct((TOP_K, N), F32),
            jax.ShapeDtypeStruct((TOP_K, N), jnp.int32),
            jax.ShapeDtypeStruct((N_EXPERTS, 128), jnp.int32),
        ],
        scratch_shapes=[
            pltpu.VMEM((T + CONV_HIST, D_CONV), F32),
            pltpu.VMEM((SUBLANES - 1, T + CONV_HIST, D_CONV), F32),
            pltpu.VMEM((T, D_MODEL), BF16),
            pltpu.VMEM((N_EXPERTS, 128), F32),
        ],
        compiler_params=pltpu.CompilerParams(
            dimension_semantics=("arbitrary", "arbitrary"), vmem_limit_bytes=VMEM_LIMIT),
        name="mix",
    )(*mix_in)

    blk = EXPERT_BLOCK
    n_blocks = (N * TOP_K) // blk + N_EXPERTS
    n_slots = n_blocks * blk
    counts = cnt[:, 0]
    nblk_e = (counts + blk - 1) // blk
    block_end = jnp.cumsum(nblk_e)
    pad_start = (block_end - nblk_e) * blk
    dest = rank
    for e in range(N_EXPERTS):
        dest = dest + jnp.where(idx == e, pad_start[e], 0)
    dest = dest * TOKEN_LINES
    blk_ids = jnp.arange(n_blocks, dtype=jnp.int32)
    block_e = jnp.minimum(
        jnp.sum((blk_ids[:, None] >= block_end[None, :]).astype(jnp.int32), axis=1),
        N_EXPERTS - 1).astype(jnp.int32)
    total = block_end[-1:].astype(jnp.int32)
    pad_lo = ((pad_start + counts) * TOKEN_LINES).astype(jnp.int32)
    pad_n = (nblk_e * blk - counts).astype(jnp.int32)
    assert N // DISPATCH_TILE >= N_EXPERTS

    xs = pl.pallas_call(
        _dispatch_kernel,
        grid=(N // DISPATCH_TILE,),
        in_specs=[
            pl.BlockSpec((TOP_K, DISPATCH_TILE), lambda i: (0, i), memory_space=pltpu.SMEM),
            pl.BlockSpec(memory_space=pltpu.SMEM),
            pl.BlockSpec(memory_space=pltpu.SMEM),
            pl.BlockSpec(memory_space=pltpu.SMEM),
            pl.BlockSpec((DISPATCH_TILE * TOKEN_LINES, LANES), lambda i: (i, 0)),
        ],
        out_specs=pl.BlockSpec(memory_space=pl.ANY),
        out_shape=jax.ShapeDtypeStruct((n_slots * TOKEN_LINES, LANES), F32),
        scratch_shapes=[pltpu.VMEM((blk * TOKEN_LINES, LANES), F32),
                        pltpu.SemaphoreType.DMA(()), pltpu.SemaphoreType.DMA(()),
                        pltpu.SemaphoreType.DMA(())],
        compiler_params=pltpu.CompilerParams(
            dimension_semantics=("arbitrary",), vmem_limit_bytes=VMEM_LIMIT),
        name="dispatch",
    )(dest, pad_lo, pad_n, total, h2)

    run_end = block_end[block_e]
    next_e = jnp.where(run_end < total[0], block_e[jnp.minimum(run_end, n_blocks - 1)], -1)
    nonempty = (nblk_e > 0).astype(jnp.int32)
    run_rank = jnp.cumsum(nonempty) - nonempty
    w_slot = (run_rank[block_e] % 2).astype(jnp.int32)
    next_e = next_e.astype(jnp.int32)

    def blk_map(i, be, tot, nxt, slot):
        return (jnp.minimum(i, tot[0] - 1), 0)

    def exp_map(i, be, tot, nxt, slot):
        return (be[jnp.minimum(i, tot[0] - 1)], 0, 0)

    y = pl.pallas_call(
        _expert_kernel,
        grid_spec=pltpu.PrefetchScalarGridSpec(
            num_scalar_prefetch=4,
            grid=(n_blocks,),
            in_specs=[
                pl.BlockSpec((blk * TOKEN_LINES, LANES), blk_map),
                pl.BlockSpec(memory_space=pl.ANY),
                pl.BlockSpec((None, 1, 2 * D_FF), exp_map),
                pl.BlockSpec(memory_space=pl.ANY),
                pl.BlockSpec((None, 1, D), exp_map),
            ],
            out_specs=pl.BlockSpec((blk * TOKEN_LINES, LANES), lambda i, be, tot, nxt, slot: (i, 0)),
            scratch_shapes=[pltpu.VMEM((2, D, 2 * D_FF), F32), pltpu.VMEM((2, D_FF, D), F32),
                            pltpu.SemaphoreType.DMA((2,)), pltpu.SemaphoreType.DMA((2,))],
        ),
        out_shape=jax.ShapeDtypeStruct((n_slots * TOKEN_LINES, LANES), F32),
        compiler_params=pltpu.CompilerParams(
            dimension_semantics=("arbitrary",), vmem_limit_bytes=VMEM_LIMIT),
        name="experts",
    )(block_e, total, next_e, w_slot, xs, w_exp1[l], b_exp1[l].reshape(N_EXPERTS, 1, 2 * D_FF),
      w_exp2[l], b_exp2[l].reshape(N_EXPERTS, 1, D))

    TC = COMBINE_TILE
    out = pl.pallas_call(
        _combine_kernel,
        grid=(N // TC,),
        in_specs=[
            pl.BlockSpec((TOP_K, TC), lambda i: (0, i), memory_space=pltpu.SMEM),
            pl.BlockSpec((TOP_K, TC), lambda i: (0, jnp.minimum(i + 1, N // TC - 1)),
                         memory_space=pltpu.SMEM),
            pl.BlockSpec((TC, D), lambda i: (i, 0)),
            pl.BlockSpec((TOP_K, TC), lambda i: (0, i)),
            _full((1, D)),
            pl.BlockSpec(memory_space=pl.ANY),
        ],
        out_specs=pl.BlockSpec((TC, D), lambda i: (i, 0)),
        out_shape=jax.ShapeDtypeStruct((N, D), F32),
        scratch_shapes=[pltpu.VMEM((2, TOP_K, TC * GATHER_PITCH, LANES), F32),
                        pltpu.SemaphoreType.DMA((2,))],
        compiler_params=pltpu.CompilerParams(
            dimension_semantics=("arbitrary",), vmem_limit_bytes=VMEM_LIMIT),
        name="combine",
    )(dest, dest, x1.reshape(N, D), gate, row(final_g), y)
    return out.reshape(B, S, D)
```

```python
import functools

import jax
import jax.numpy as jnp
from jax import lax
from jax.experimental import pallas as pl
from jax.experimental.pallas import tpu as pltpu

D_MODEL = 1024
D_CONV = 512
D_GMLP = 512
CONV_WIDTH = 31
CONV_HIST = 32
GMLP_HEADS = 8
GMLP_HEAD_DIM = 64
GMLP_BLOCK = 128
CHUNK = 64
N_EXPERTS = 32
TOP_K = 4
D_FF = 1024
SWIGLU_ALPHA = 1.702
SWIGLU_LIMIT = 7.0
EPS = 1e-5

MIX_TILE = 512
CONV_ROWS = 32
SUBLANES = 8
SHIFT_ROWS = 136
EXPERT_BLOCK = 512
DISPATCH_TILE = 1024
COMBINE_TILE = 512
ISSUE_UNROLL = 8
PAD_BITS = tuple(1 << b for b in reversed(range(EXPERT_BLOCK.bit_length() - 1)))
VMEM_LIMIT = 56 * 1024 * 1024

F32 = jnp.float32
BF16 = jnp.bfloat16


def _rms(x, g):
    return x * lax.rsqrt(jnp.mean(x * x, axis=-1, keepdims=True) + EPS) * g


def _ln(x, g, b):
    mu = jnp.mean(x, axis=-1, keepdims=True)
    xc = x - mu
    var = jnp.mean(xc * xc, axis=-1, keepdims=True)
    return xc * lax.rsqrt(var + EPS) * g + b


def _gelu(x):
    return 0.5 * x * (1.0 + lax.erf(x * (2.0 ** -0.5)))


LANES = 128
TOKEN_LINES = D_MODEL // LANES


GATHER_PITCH = TOKEN_LINES + 1


def _load_token_tiles(ref, rows, lead=(), pitch=TOKEN_LINES):
    chunks = [ref[lead + (pl.ds(c, rows, stride=pitch), slice(None))] for c in range(TOKEN_LINES)]
    return jnp.concatenate(chunks, axis=1)


def _store_token_tiles(ref, value, rows):
    for c in range(TOKEN_LINES):
        ref[pl.ds(c, rows, stride=TOKEN_LINES), :] = value[:, c * LANES:(c + 1) * LANES]


def _mix_kernel(x_ref, g1_ref, win_ref, bin_ref, cw_ref, cb_ref, clg_ref, clb_ref,
                slg_ref, slb_ref, sgw_ref, sgb_ref, ggc_ref, ggs_ref, wout_ref, bout_ref,
                g2_ref, wr_ref, br_ref,
                x1_ref, h2_ref, idx_ref, gate_ref, rank_ref, cnt_ref,
                ubuf, ushift, ycat, cnt_s):
    T = MIX_TILE
    b = pl.program_id(0)
    s = pl.program_id(1)

    @pl.when(s == 0)
    def _():
        ubuf[0:CONV_HIST, :] = jnp.zeros((CONV_HIST, D_CONV), F32)

    @pl.when((b == 0) & (s == 0))
    def _():
        cnt_s[...] = jnp.zeros_like(cnt_s)

    x = x_ref[...]
    h = _rms(x, g1_ref[...])
    p = jnp.dot(h.astype(BF16), win_ref[...], preferred_element_type=F32) + bin_ref[...]

    u = p[:, 0:D_CONV] * jax.nn.sigmoid(p[:, D_CONV:2 * D_CONV])
    ubuf[CONV_HIST:CONV_HIST + T, :] = u

    n_rows = T + CONV_HIST
    for r in range(1, SUBLANES):
        ushift[r - 1] = pltpu.roll(ubuf[...], n_rows - r, axis=0)

    for c in range(T // CONV_ROWS):
        r0 = c * CONV_ROWS
        acc = jnp.broadcast_to(cb_ref[...], (CONV_ROWS, D_CONV))
        for k in range(CONV_WIDTH):
            off = CONV_HIST - (CONV_WIDTH - 1) + k
            q, r = divmod(off, SUBLANES)
            a0 = r0 + q * SUBLANES
            if r == 0:
                tap = ubuf[a0:a0 + CONV_ROWS, :]
            else:
                tap = ushift[r - 1, a0:a0 + CONV_ROWS, :]
            w8 = cw_ref[k * SUBLANES:(k + 1) * SUBLANES, :]
            acc = acc + jnp.concatenate([w8] * (CONV_ROWS // SUBLANES), axis=0) * tap
        y = _ln(acc, clg_ref[...], clb_ref[...])
        y = y * jax.nn.sigmoid(y)
        y = _rms(y, ggc_ref[...])
        ycat[r0:r0 + CONV_ROWS, 0:D_CONV] = y.astype(BF16)
    ubuf[0:CONV_HIST, :] = ubuf[T:T + CONV_HIST, :]

    uu = _gelu(p[:, 2 * D_CONV:2 * D_CONV + D_GMLP])
    vv = _ln(_gelu(p[:, 2 * D_CONV + D_GMLP:]), slg_ref[...], slb_ref[...])
    t_out = lax.broadcasted_iota(jnp.int32, (GMLP_BLOCK, 2 * GMLP_BLOCK), 0)
    s_in = lax.broadcasted_iota(jnp.int32, (GMLP_BLOCK, 2 * GMLP_BLOCK), 1) % GMLP_BLOCK
    wmask = (s_in // CHUNK) <= (t_out // CHUNK)
    lane = lax.broadcasted_iota(jnp.int32, (GMLP_BLOCK, 2 * GMLP_HEAD_DIM), 1)
    lo = lane < GMLP_HEAD_DIM
    n_pair = GMLP_HEADS // 2
    wcat = [jnp.where(wmask, sgw_ref[j], 0.0).astype(BF16) for j in range(n_pair)]
    for blk in range(T // GMLP_BLOCK):
        rows = slice(blk * GMLP_BLOCK, (blk + 1) * GMLP_BLOCK)
        outs = []
        for j in range(n_pair):
            cols = slice(j * 128, (j + 1) * 128)
            vp = vv[rows, cols]
            vstack = jnp.concatenate(
                [jnp.where(lo, vp, 0.0), jnp.where(lo, 0.0, vp)], axis=0).astype(BF16)
            mixed = jnp.dot(wcat[j], vstack, preferred_element_type=F32) + sgb_ref[:, cols]
            outs.append(uu[rows, cols] * mixed)
        ysg = jnp.concatenate(outs, axis=1)
        ycat[rows, D_CONV:] = _rms(ysg, ggs_ref[...]).astype(BF16)

    o = jnp.dot(ycat[...], wout_ref[...], preferred_element_type=F32) + bout_ref[...]
    x1 = x + o
    x1_ref[...] = x1
    h2 = _rms(x1, g2_ref[...])
    _store_token_tiles(h2_ref, h2, T)

    logits = lax.dot_general(wr_ref[...].astype(BF16), h2.astype(BF16),
                             (((1,), (1,)), ((), ())),
                             preferred_element_type=F32) + br_ref[...]
    e_iota = lax.broadcasted_iota(jnp.int32, (N_EXPERTS, T), 0)
    work = logits
    vals, idxs, sels = [], [], []
    for _k in range(TOP_K):
        m = jnp.max(work, axis=0, keepdims=True)
        am = jnp.min(jnp.where(work == m, e_iota, N_EXPERTS), axis=0, keepdims=True)
        sel = e_iota == am
        vals.append(m)
        idxs.append(am)
        sels.append(sel)
        work = jnp.where(sel, -jnp.inf, work)
    exps = [jnp.exp(v - vals[0]) for v in vals]
    denom = exps[0] + exps[1] + exps[2] + exps[3]
    gate_ref[...] = jnp.concatenate([e / denom for e in exps], axis=0)
    idx_ref[...] = jnp.concatenate(idxs, axis=0)

    member = jnp.zeros((N_EXPERTS, T), F32)
    for sel in sels:
        member = member + jnp.where(sel, 1.0, 0.0)
    r_i = lax.broadcasted_iota(jnp.int32, (T, T), 0)
    c_i = lax.broadcasted_iota(jnp.int32, (T, T), 1)
    upper = jnp.where(r_i < c_i, 1.0, 0.0).astype(BF16)
    before = jnp.dot(member.astype(BF16), upper, preferred_element_type=F32)
    before = before + cnt_s[:, 0:1]
    ranks = [jnp.sum(jnp.where(sel, before, 0.0), axis=0, keepdims=True) for sel in sels]
    rank_ref[...] = jnp.concatenate(ranks, axis=0).astype(jnp.int32)
    cnt_new = cnt_s[...] + jnp.sum(member, axis=1, keepdims=True)
    cnt_s[...] = cnt_new
    cnt_ref[...] = cnt_new.astype(jnp.int32)


def _dispatch_kernel(dest_ref, padlo_ref, padn_ref, tot_ref, h2_ref, xs_ref, zbuf, sem, zsem, tsem):
    i = pl.program_id(0)
    blk_lines = EXPERT_BLOCK * TOKEN_LINES
    n_blocks = xs_ref.shape[0] // blk_lines

    def pad_pieces(e, op):
        n = padn_ref[e]
        line = padlo_ref[e]
        for bit in PAD_BITS:
            hit = (n & bit) != 0

            @pl.when(hit)
            def _(line=line, bit=bit):
                op(pltpu.make_async_copy(
                    zbuf.at[pl.ds(0, bit * TOKEN_LINES)],
                    xs_ref.at[pl.ds(pl.multiple_of(line, TOKEN_LINES), bit * TOKEN_LINES)], zsem))

            line = line + jnp.where(hit, bit * TOKEN_LINES, 0)

    def for_each_expert(op):
        def body(e, carry):
            pad_pieces(e, op)
            return carry
        lax.fori_loop(0, N_EXPERTS, body, 0)

    @pl.when(i == 0)
    def _():
        zbuf[...] = jnp.zeros_like(zbuf)
        for_each_expert(lambda c: c.start())

    tail_blk = tot_ref[0] + i

    def tail_copy():
        line = pl.multiple_of(tail_blk * blk_lines, blk_lines)
        return pltpu.make_async_copy(zbuf, xs_ref.at[pl.ds(line, blk_lines)], tsem)

    @pl.when(tail_blk < n_blocks)
    def _():
        tail_copy().start()

    def start(g, carry):
        for j in range(ISSUE_UNROLL):
            t = g * ISSUE_UNROLL + j
            src = h2_ref.at[pl.ds(pl.multiple_of(t * TOKEN_LINES, TOKEN_LINES), TOKEN_LINES)]
            for k in range(TOP_K):
                line = pl.multiple_of(dest_ref[k, t], TOKEN_LINES)
                pltpu.make_async_copy(src, xs_ref.at[pl.ds(line, TOKEN_LINES)], sem).start(priority=k % 2)
        return carry

    lax.fori_loop(0, DISPATCH_TILE // ISSUE_UNROLL, start, 0)
    for k in range(TOP_K):
        pltpu.make_async_copy(h2_ref, xs_ref.at[pl.ds(0, DISPATCH_TILE * TOKEN_LINES)], sem).wait()

    @pl.when(i == 0)
    def _():
        for_each_expert(lambda c: c.wait())

    @pl.when(tail_blk < n_blocks)
    def _():
        tail_copy().wait()


def _expert_kernel(be_ref, tot_ref, nxt_ref, slot_ref, xs_ref, w1_hbm, b1_ref, w2_hbm, b2_ref, y_ref,
                   w1buf, w2buf, wsem1, wsem2):
    i = pl.program_id(0)

    def fetch(e, s):
        return (pltpu.make_async_copy(w1_hbm.at[e], w1buf.at[s], wsem1.at[s]),
                pltpu.make_async_copy(w2_hbm.at[e], w2buf.at[s], wsem2.at[s]))

    @pl.when(i < tot_ref[0])
    def _():
        e = be_ref[i]
        s = slot_ref[i]

        @pl.when(i == 0)
        def _():
            for cp in fetch(e, s):
                cp.start()

        @pl.when((i == 0) | (e != be_ref[jnp.maximum(i - 1, 0)]))
        def _():
            for cp in fetch(e, s):
                cp.wait()

            @pl.when(nxt_ref[i] >= 0)
            def _():
                for cp in fetch(nxt_ref[i], 1 - s):
                    cp.start()

        xb = _load_token_tiles(xs_ref, EXPERT_BLOCK).astype(BF16)
        hcat = lax.dot_general(xb, w1buf[s], (((1,), (0,)), ((), ())),
                               preferred_element_type=F32) + b1_ref[...]
        x_glu = jnp.minimum(hcat[:, :D_FF], SWIGLU_LIMIT)
        x_lin = jnp.clip(hcat[:, D_FF:], -SWIGLU_LIMIT, SWIGLU_LIMIT)
        act = x_glu * jax.nn.sigmoid(SWIGLU_ALPHA * x_glu) * (x_lin + 1.0)
        y = lax.dot_general(act.astype(BF16), w2buf[s], (((1,), (0,)), ((), ())),
                            preferred_element_type=F32) + b2_ref[...]
        _store_token_tiles(y_ref, y, EXPERT_BLOCK)

    @pl.when(i >= tot_ref[0])
    def _():
        y_ref[...] = jnp.zeros_like(y_ref)


def _combine_kernel(dest_ref, dnext_ref, x1_ref, gate_ref, fg_ref, y_ref, out_ref, buf, sem):
    TC = COMBINE_TILE
    i = pl.program_id(0)
    cur = i % 2
    nxt = (i + 1) % 2

    def issue(d_ref, slot):
        def start(g, carry):
            for j in range(ISSUE_UNROLL):
                t = g * ISSUE_UNROLL + j
                row = t * GATHER_PITCH
                for k in range(TOP_K):
                    line = pl.multiple_of(d_ref[k, t], TOKEN_LINES)
                    pltpu.make_async_copy(
                        y_ref.at[pl.ds(line, TOKEN_LINES)],
                        buf.at[slot, k, pl.ds(row, TOKEN_LINES)],
                        sem.at[slot]).start(priority=k % 2)
            return carry

        lax.fori_loop(0, TC // ISSUE_UNROLL, start, 0)

    @pl.when(i == 0)
    def _():
        issue(dest_ref, cur)

    @pl.when(i + 1 < pl.num_programs(0))
    def _():
        issue(dnext_ref, nxt)

    for k in range(TOP_K):
        pltpu.make_async_copy(y_ref.at[pl.ds(0, TC * TOKEN_LINES)],
                              buf.at[cur, k, pl.ds(0, TC * TOKEN_LINES)], sem.at[cur]).wait()

    gpad = jnp.concatenate([gate_ref[...], jnp.zeros((128 - TOP_K, TC), F32)], axis=0)
    gt = gpad.T
    acc = x1_ref[...]
    for k in range(TOP_K):
        acc = acc + gt[:, k:k + 1] * _load_token_tiles(buf, TC, lead=(cur, k), pitch=GATHER_PITCH)
    out_ref[...] = _rms(acc, fg_ref[...])


def _full(shape):
    n = len(shape)
    return pl.BlockSpec(shape, lambda *_: (0,) * n)


def kernel(x, norm1_g, w_in, b_in, conv_w, conv_b, conv_ln_g, conv_ln_b, sg_ln_g, sg_ln_b,
           sg_w, sg_b, grp_g_conv, grp_g_sg, w_out, b_out, norm2_g, w_router, b_router,
           w_exp1, b_exp1, w_exp2, b_exp2, final_g):
    B, S, D = x.shape
    assert D == D_MODEL and S % MIX_TILE == 0 and norm1_g.shape[0] == 1
    N = B * S
    T = MIX_TILE
    n_s = S // T
    l = 0

    row = lambda v: v.reshape(1, -1).astype(F32)
    cw = jnp.repeat(conv_w[l].astype(F32), SUBLANES, axis=0)
    sgw = sg_w[l].reshape(GMLP_HEADS // 2, 2, GMLP_BLOCK, GMLP_BLOCK)
    sgw = sgw.transpose(0, 2, 1, 3).reshape(GMLP_HEADS // 2, GMLP_BLOCK, 2 * GMLP_BLOCK)
    sgb = jnp.repeat(sg_b[l].T, GMLP_HEAD_DIM, axis=1)
    wr_t = w_router[l].T
    br = b_router[l].reshape(N_EXPERTS, 1)

    mix_in = [
        x, row(norm1_g[l]), w_in[l].astype(BF16), row(b_in[l]), cw, row(conv_b[l]),
        row(conv_ln_g[l]), row(conv_ln_b[l]), row(sg_ln_g[l]), row(sg_ln_b[l]), sgw, sgb,
        row(grp_g_conv[l]), row(grp_g_sg[l]), w_out[l].astype(BF16), row(b_out[l]),
        row(norm2_g[l]), wr_t, br,
    ]
    mix_specs = [pl.BlockSpec((None, T, D), lambda b, s: (b, s, 0))]
    mix_specs += [_full(a.shape) for a in mix_in[1:]]
    tok_spec = pl.BlockSpec((TOP_K, T), lambda b, s: (0, b * n_s + s))
    x1, h2, idx, gate, rank, cnt = pl.pallas_call(
        _mix_kernel,
        grid=(B, n_s),
        in_specs=mix_specs,
        out_specs=[
            pl.BlockSpec((None, T, D), lambda b, s: (b, s, 0)),
            pl.BlockSpec((T * TOKEN_LINES, LANES), lambda b, s: (b * n_s + s, 0)),
            tok_spec, tok_spec, tok_spec,
            pl.BlockSpec((N_EXPERTS, 128), lambda b, s: (0, 0)),
        ],
        out_shape=[
            jax.ShapeDtypeStruct((B, S, D), F32),
            jax.ShapeDtypeStruct((N * TOKEN_LINES, LANES), F32),
            jax.ShapeDtypeStruct((TOP_K, N), jnp.int32),
            jax.ShapeDtypeStruct((TOP_K, N), F32),
            jax.ShapeDtypeStruct((TOP_K, N), jnp.int32),
            jax.ShapeDtypeStruct((N_EXPERTS, 128), jnp.int32),
        ],
        scratch_shapes=[
            pltpu.VMEM((T + CONV_HIST, D_CONV), F32),
            pltpu.VMEM((SUBLANES - 1, T + CONV_HIST, D_CONV), F32),
            pltpu.VMEM((T, D_MODEL), BF16),
            pltpu.VMEM((N_EXPERTS, 128), F32),
        ],
        compiler_params=pltpu.CompilerParams(
            dimension_semantics=("arbitrary", "arbitrary"), vmem_limit_bytes=VMEM_LIMIT),
        name="mix",
    )(*mix_in)

    blk = EXPERT_BLOCK
    n_blocks = (N * TOP_K) // blk + N_EXPERTS
    n_slots = n_blocks * blk
    counts = cnt[:, 0]
    nblk_e = (counts + blk - 1) // blk
    block_end = jnp.cumsum(nblk_e)
    pad_start = (block_end - nblk_e) * blk
    dest = rank
    for e in range(N_EXPERTS):
        dest = dest + jnp.where(idx == e, pad_start[e], 0)
    dest = dest * TOKEN_LINES
    blk_ids = jnp.arange(n_blocks, dtype=jnp.int32)
    block_e = jnp.minimum(
        jnp.sum((blk_ids[:, None] >= block_end[None, :]).astype(jnp.int32), axis=1),
        N_EXPERTS - 1).astype(jnp.int32)
    total = block_end[-1:].astype(jnp.int32)
    pad_lo = ((pad_start + counts) * TOKEN_LINES).astype(jnp.int32)
    pad_n = (nblk_e * blk - counts).astype(jnp.int32)
    assert N // DISPATCH_TILE >= N_EXPERTS

    xs = pl.pallas_call(
        _dispatch_kernel,
        grid=(N // DISPATCH_TILE,),
        in_specs=[
            pl.BlockSpec((TOP_K, DISPATCH_TILE), lambda i: (0, i), memory_space=pltpu.SMEM),
            pl.BlockSpec(memory_space=pltpu.SMEM),
            pl.BlockSpec(memory_space=pltpu.SMEM),
            pl.BlockSpec(memory_space=pltpu.SMEM),
            pl.BlockSpec((DISPATCH_TILE * TOKEN_LINES, LANES), lambda i: (i, 0)),
        ],
        out_specs=pl.BlockSpec(memory_space=pl.ANY),
        out_shape=jax.ShapeDtypeStruct((n_slots * TOKEN_LINES, LANES), F32),
        scratch_shapes=[pltpu.VMEM((blk * TOKEN_LINES, LANES), F32),
                        pltpu.SemaphoreType.DMA(()), pltpu.SemaphoreType.DMA(()),
                        pltpu.SemaphoreType.DMA(())],
        compiler_params=pltpu.CompilerParams(
            dimension_semantics=("arbitrary",), vmem_limit_bytes=VMEM_LIMIT),
        name="dispatch",
    )(dest, pad_lo, pad_n, total, h2)

    e_ids = jnp.arange(N_EXPERTS, dtype=jnp.int32)[None, :]
    owner = block_e[:, None] == e_ids
    run_end = jnp.sum(jnp.where(owner, block_end[None, :], 0), axis=1)
    next_owner = jnp.sum((run_end[:, None] >= block_end[None, :]).astype(jnp.int32), axis=1)
    next_e = jnp.where(run_end < total[0], next_owner, -1).astype(jnp.int32)
    nonempty = (nblk_e > 0).astype(jnp.int32)
    run_rank = jnp.sum(jnp.where(e_ids < block_e[:, None], nonempty[None, :], 0), axis=1)
    w_slot = (run_rank % 2).astype(jnp.int32)

    def blk_map(i, be, tot, nxt, slot):
        return (jnp.minimum(i, tot[0] - 1), 0)

    def exp_map(i, be, tot, nxt, slot):
        return (be[jnp.minimum(i, tot[0] - 1)], 0, 0)

    y = pl.pallas_call(
        _expert_kernel,
        grid_spec=pltpu.PrefetchScalarGridSpec(
            num_scalar_prefetch=4,
            grid=(n_blocks,),
            in_specs=[
                pl.BlockSpec((blk * TOKEN_LINES, LANES), blk_map),
                pl.BlockSpec(memory_space=pl.ANY),
                pl.BlockSpec((None, 1, 2 * D_FF), exp_map),
                pl.BlockSpec(memory_space=pl.ANY),
                pl.BlockSpec((None, 1, D), exp_map),
            ],
            out_specs=pl.BlockSpec((blk * TOKEN_LINES, LANES), lambda i, be, tot, nxt, slot: (i, 0)),
            scratch_shapes=[pltpu.VMEM((2, D, 2 * D_FF), F32), pltpu.VMEM((2, D_FF, D), F32),
                            pltpu.SemaphoreType.DMA((2,)), pltpu.SemaphoreType.DMA((2,))],
        ),
        out_shape=jax.ShapeDtypeStruct((n_slots * TOKEN_LINES, LANES), F32),
        compiler_params=pltpu.CompilerParams(
            dimension_semantics=("arbitrary",), vmem_limit_bytes=VMEM_LIMIT),
        name="experts",
    )(block_e, total, next_e, w_slot, xs, w_exp1[l], b_exp1[l].reshape(N_EXPERTS, 1, 2 * D_FF),
      w_exp2[l], b_exp2[l].reshape(N_EXPERTS, 1, D))

    TC = COMBINE_TILE
    out = pl.pallas_call(
        _combine_kernel,
        grid=(N // TC,),
        in_specs=[
            pl.BlockSpec((TOP_K, TC), lambda i: (0, i), memory_space=pltpu.SMEM),
            pl.BlockSpec((TOP_K, TC), lambda i: (0, jnp.minimum(i + 1, N // TC - 1)),
                         memory_space=pltpu.SMEM),
            pl.BlockSpec((TC, D), lambda i: (i, 0)),
            pl.BlockSpec((TOP_K, TC), lambda i: (0, i)),
            _full((1, D)),
            pl.BlockSpec(memory_space=pl.ANY),
        ],
        out_specs=pl.BlockSpec((TC, D), lambda i: (i, 0)),
        out_shape=jax.ShapeDtypeStruct((N, D), F32),
        scratch_shapes=[pltpu.VMEM((2, TOP_K, TC * GATHER_PITCH, LANES), F32),
                        pltpu.SemaphoreType.DMA((2,))],
        compiler_params=pltpu.CompilerParams(
            dimension_semantics=("arbitrary",), vmem_limit_bytes=VMEM_LIMIT),
        name="combine",
    )(dest, dest, x1.reshape(N, D), gate, row(final_g), y)
    return out.reshape(B, S, D)
```

```python
import functools

import jax
import jax.numpy as jnp
from jax import lax
from jax.experimental import pallas as pl
from jax.experimental.pallas import tpu as pltpu

D_MODEL = 1024
D_CONV = 512
D_GMLP = 512
CONV_WIDTH = 31
CONV_HIST = 32
GMLP_HEADS = 8
GMLP_HEAD_DIM = 64
GMLP_BLOCK = 128
CHUNK = 64
N_EXPERTS = 32
TOP_K = 4
D_FF = 1024
SWIGLU_ALPHA = 1.702
SWIGLU_LIMIT = 7.0
EPS = 1e-5

MIX_TILE = 512
CONV_ROWS = 32
SUBLANES = 8
SHIFT_ROWS = 136
EXPERT_BLOCK = 512
DISPATCH_TILE = 1024
COMBINE_TILE = 512
ISSUE_UNROLL = 16
PAD_BITS = tuple(1 << b for b in reversed(range(EXPERT_BLOCK.bit_length() - 1)))
VMEM_LIMIT = 56 * 1024 * 1024

F32 = jnp.float32
BF16 = jnp.bfloat16


def _rms(x, g):
    return x * lax.rsqrt(jnp.mean(x * x, axis=-1, keepdims=True) + EPS) * g


def _ln(x, g, b):
    mu = jnp.mean(x, axis=-1, keepdims=True)
    xc = x - mu
    var = jnp.mean(xc * xc, axis=-1, keepdims=True)
    return xc * lax.rsqrt(var + EPS) * g + b


def _gelu(x):
    return 0.5 * x * (1.0 + lax.erf(x * (2.0 ** -0.5)))


LANES = 128
TOKEN_LINES = D_MODEL // LANES


GATHER_PITCH = TOKEN_LINES + 1


def _load_token_tiles(ref, rows, lead=(), pitch=TOKEN_LINES):
    chunks = [ref[lead + (pl.ds(c, rows, stride=pitch), slice(None))] for c in range(TOKEN_LINES)]
    return jnp.concatenate(chunks, axis=1)


def _store_token_tiles(ref, value, rows):
    for c in range(TOKEN_LINES):
        ref[pl.ds(c, rows, stride=TOKEN_LINES), :] = value[:, c * LANES:(c + 1) * LANES]


def _mix_kernel(x_ref, g1_ref, win_ref, bin_ref, cw_ref, cb_ref, clg_ref, clb_ref,
                slg_ref, slb_ref, sgw_ref, sgb_ref, ggc_ref, ggs_ref, wout_ref, bout_ref,
                g2_ref, wr_ref, br_ref,
                x1_ref, h2_ref, idx_ref, gate_ref, rank_ref, cnt_ref,
                ubuf, ushift, ycat, cnt_s):
    T = MIX_TILE
    b = pl.program_id(0)
    s = pl.program_id(1)

    @pl.when(s == 0)
    def _():
        ubuf[0:CONV_HIST, :] = jnp.zeros((CONV_HIST, D_CONV), F32)

    @pl.when((b == 0) & (s == 0))
    def _():
        cnt_s[...] = jnp.zeros_like(cnt_s)

    x = x_ref[...]
    h = _rms(x, g1_ref[...])
    p = jnp.dot(h.astype(BF16), win_ref[...], preferred_element_type=F32) + bin_ref[...]

    u = p[:, 0:D_CONV] * jax.nn.sigmoid(p[:, D_CONV:2 * D_CONV])
    ubuf[CONV_HIST:CONV_HIST + T, :] = u

    n_rows = T + CONV_HIST
    for r in range(1, SUBLANES):
        ushift[r - 1] = pltpu.roll(ubuf[...], n_rows - r, axis=0)

    for c in range(T // CONV_ROWS):
        r0 = c * CONV_ROWS
        acc = jnp.broadcast_to(cb_ref[...], (CONV_ROWS, D_CONV))
        for k in range(CONV_WIDTH):
            off = CONV_HIST - (CONV_WIDTH - 1) + k
            q, r = divmod(off, SUBLANES)
            a0 = r0 + q * SUBLANES
            if r == 0:
                tap = ubuf[a0:a0 + CONV_ROWS, :]
            else:
                tap = ushift[r - 1, a0:a0 + CONV_ROWS, :]
            w8 = cw_ref[k * SUBLANES:(k + 1) * SUBLANES, :]
            acc = acc + jnp.concatenate([w8] * (CONV_ROWS // SUBLANES), axis=0) * tap
        y = _ln(acc, clg_ref[...], clb_ref[...])
        y = y * jax.nn.sigmoid(y)
        y = _rms(y, ggc_ref[...])
        ycat[r0:r0 + CONV_ROWS, 0:D_CONV] = y.astype(BF16)
    ubuf[0:CONV_HIST, :] = ubuf[T:T + CONV_HIST, :]

    uu = _gelu(p[:, 2 * D_CONV:2 * D_CONV + D_GMLP])
    vv = _ln(_gelu(p[:, 2 * D_CONV + D_GMLP:]), slg_ref[...], slb_ref[...])
    t_out = lax.broadcasted_iota(jnp.int32, (GMLP_BLOCK, 2 * GMLP_BLOCK), 0)
    s_in = lax.broadcasted_iota(jnp.int32, (GMLP_BLOCK, 2 * GMLP_BLOCK), 1) % GMLP_BLOCK
    wmask = (s_in // CHUNK) <= (t_out // CHUNK)
    lane = lax.broadcasted_iota(jnp.int32, (GMLP_BLOCK, 2 * GMLP_HEAD_DIM), 1)
    lo = lane < GMLP_HEAD_DIM
    n_pair = GMLP_HEADS // 2
    wcat = [jnp.where(wmask, sgw_ref[j], 0.0).astype(BF16) for j in range(n_pair)]
    for blk in range(T // GMLP_BLOCK):
        rows = slice(blk * GMLP_BLOCK, (blk + 1) * GMLP_BLOCK)
        outs = []
        for j in range(n_pair):
            cols = slice(j * 128, (j + 1) * 128)
            vp = vv[rows, cols]
            vstack = jnp.concatenate(
                [jnp.where(lo, vp, 0.0), jnp.where(lo, 0.0, vp)], axis=0).astype(BF16)
            mixed = jnp.dot(wcat[j], vstack, preferred_element_type=F32) + sgb_ref[:, cols]
            outs.append(uu[rows, cols] * mixed)
        ysg = jnp.concatenate(outs, axis=1)
        ycat[rows, D_CONV:] = _rms(ysg, ggs_ref[...]).astype(BF16)

    o = jnp.dot(ycat[...], wout_ref[...], preferred_element_type=F32) + bout_ref[...]
    x1 = x + o
    x1_ref[...] = x1
    h2 = _rms(x1, g2_ref[...])
    _store_token_tiles(h2_ref, h2, T)

    logits = lax.dot_general(wr_ref[...].astype(BF16), h2.astype(BF16),
                             (((1,), (1,)), ((), ())),
                             preferred_element_type=F32) + br_ref[...]
    e_iota = lax.broadcasted_iota(jnp.int32, (N_EXPERTS, T), 0)
    work = logits
    vals, idxs, sels = [], [], []
    for _k in range(TOP_K):
        m = jnp.max(work, axis=0, keepdims=True)
        am = jnp.min(jnp.where(work == m, e_iota, N_EXPERTS), axis=0, keepdims=True)
        sel = e_iota == am
        vals.append(m)
        idxs.append(am)
        sels.append(sel)
        work = jnp.where(sel, -jnp.inf, work)
    exps = [jnp.exp(v - vals[0]) for v in vals]
    denom = exps[0] + exps[1] + exps[2] + exps[3]
    gate_ref[...] = jnp.concatenate([e / denom for e in exps], axis=0)
    idx_ref[...] = jnp.concatenate(idxs, axis=0)

    member = jnp.zeros((N_EXPERTS, T), F32)
    for sel in sels:
        member = member + jnp.where(sel, 1.0, 0.0)
    r_i = lax.broadcasted_iota(jnp.int32, (T, T), 0)
    c_i = lax.broadcasted_iota(jnp.int32, (T, T), 1)
    upper = jnp.where(r_i < c_i, 1.0, 0.0).astype(BF16)
    before = jnp.dot(member.astype(BF16), upper, preferred_element_type=F32)
    before = before + cnt_s[:, 0:1]
    ranks = [jnp.sum(jnp.where(sel, before, 0.0), axis=0, keepdims=True) for sel in sels]
    rank_ref[...] = jnp.concatenate(ranks, axis=0).astype(jnp.int32)
    cnt_new = cnt_s[...] + jnp.sum(member, axis=1, keepdims=True)
    cnt_s[...] = cnt_new
    cnt_ref[...] = cnt_new.astype(jnp.int32)


def _dispatch_kernel(dest_ref, padlo_ref, padn_ref, tot_ref, h2_ref, xs_ref, zbuf, sem, zsem, tsem):
    i = pl.program_id(0)
    blk_lines = EXPERT_BLOCK * TOKEN_LINES
    n_blocks = xs_ref.shape[0] // blk_lines

    def pad_pieces(e, op):
        n = padn_ref[e]
        line = padlo_ref[e]
        for bit in PAD_BITS:
            hit = (n & bit) != 0

            @pl.when(hit)
            def _(line=line, bit=bit):
                op(pltpu.make_async_copy(
                    zbuf.at[pl.ds(0, bit * TOKEN_LINES)],
                    xs_ref.at[pl.ds(pl.multiple_of(line, TOKEN_LINES), bit * TOKEN_LINES)], zsem))

            line = line + jnp.where(hit, bit * TOKEN_LINES, 0)

    def for_each_expert(op):
        def body(e, carry):
            pad_pieces(e, op)
            return carry
        lax.fori_loop(0, N_EXPERTS, body, 0)

    @pl.when(i == 0)
    def _():
        zbuf[...] = jnp.zeros_like(zbuf)
        for_each_expert(lambda c: c.start())

    tail_blk = tot_ref[0] + i

    def tail_copy():
        line = pl.multiple_of(tail_blk * blk_lines, blk_lines)
        return pltpu.make_async_copy(zbuf, xs_ref.at[pl.ds(line, blk_lines)], tsem)

    @pl.when(tail_blk < n_blocks)
    def _():
        tail_copy().start()

    def start(g, carry):
        for j in range(ISSUE_UNROLL):
            t = g * ISSUE_UNROLL + j
            src = h2_ref.at[pl.ds(pl.multiple_of(t * TOKEN_LINES, TOKEN_LINES), TOKEN_LINES)]
            for k in range(TOP_K):
                line = pl.multiple_of(dest_ref[k, t], TOKEN_LINES)
                pltpu.make_async_copy(src, xs_ref.at[pl.ds(line, TOKEN_LINES)], sem).start(priority=k % 2)
        return carry

    lax.fori_loop(0, DISPATCH_TILE // ISSUE_UNROLL, start, 0)
    for k in range(TOP_K):
        pltpu.make_async_copy(h2_ref, xs_ref.at[pl.ds(0, DISPATCH_TILE * TOKEN_LINES)], sem).wait()

    @pl.when(i == 0)
    def _():
        for_each_expert(lambda c: c.wait())

    @pl.when(tail_blk < n_blocks)
    def _():
        tail_copy().wait()


def _expert_kernel(be_ref, tot_ref, nxt_ref, slot_ref, xs_ref, w1_hbm, b1_ref, w2_hbm, b2_ref, y_ref,
                   w1buf, w2buf, wsem1, wsem2):
    i = pl.program_id(0)

    def fetch(e, s):
        return (pltpu.make_async_copy(w1_hbm.at[e], w1buf.at[s], wsem1.at[s]),
                pltpu.make_async_copy(w2_hbm.at[e], w2buf.at[s], wsem2.at[s]))

    @pl.when(i < tot_ref[0])
    def _():
        e = be_ref[i]
        s = slot_ref[i]

        @pl.when(i == 0)
        def _():
            for cp in fetch(e, s):
                cp.start()

        @pl.when((i == 0) | (e != be_ref[jnp.maximum(i - 1, 0)]))
        def _():
            for cp in fetch(e, s):
                cp.wait()

            @pl.when(nxt_ref[i] >= 0)
            def _():
                for cp in fetch(nxt_ref[i], 1 - s):
                    cp.start()

        xb = _load_token_tiles(xs_ref, EXPERT_BLOCK).astype(BF16)
        hcat = lax.dot_general(xb, w1buf[s], (((1,), (0,)), ((), ())),
                               preferred_element_type=F32) + b1_ref[...]
        x_glu = jnp.minimum(hcat[:, :D_FF], SWIGLU_LIMIT)
        x_lin = jnp.clip(hcat[:, D_FF:], -SWIGLU_LIMIT, SWIGLU_LIMIT)
        act = x_glu * jax.nn.sigmoid(SWIGLU_ALPHA * x_glu) * (x_lin + 1.0)
        y = lax.dot_general(act.astype(BF16), w2buf[s], (((1,), (0,)), ((), ())),
                            preferred_element_type=F32) + b2_ref[...]
        _store_token_tiles(y_ref, y, EXPERT_BLOCK)

    @pl.when(i >= tot_ref[0])
    def _():
        y_ref[...] = jnp.zeros_like(y_ref)


def _combine_kernel(dest_ref, dnext_ref, x1_ref, gate_ref, fg_ref, y_ref, out_ref, buf, sem):
    TC = COMBINE_TILE
    i = pl.program_id(0)
    cur = i % 2
    nxt = (i + 1) % 2

    def issue(d_ref, slot):
        def start(g, carry):
            for j in range(ISSUE_UNROLL):
                t = g * ISSUE_UNROLL + j
                row = t * GATHER_PITCH
                for k in range(TOP_K):
                    line = pl.multiple_of(d_ref[k, t], TOKEN_LINES)
                    pltpu.make_async_copy(
                        y_ref.at[pl.ds(line, TOKEN_LINES)],
                        buf.at[slot, k, pl.ds(row, TOKEN_LINES)],
                        sem.at[slot]).start(priority=k % 2)
            return carry

        lax.fori_loop(0, TC // ISSUE_UNROLL, start, 0)

    @pl.when(i == 0)
    def _():
        issue(dest_ref, cur)

    @pl.when(i + 1 < pl.num_programs(0))
    def _():
        issue(dnext_ref, nxt)

    for k in range(TOP_K):
        pltpu.make_async_copy(y_ref.at[pl.ds(0, TC * TOKEN_LINES)],
                              buf.at[cur, k, pl.ds(0, TC * TOKEN_LINES)], sem.at[cur]).wait()

    gpad = jnp.concatenate([gate_ref[...], jnp.zeros((128 - TOP_K, TC), F32)], axis=0)
    gt = gpad.T
    acc = x1_ref[...]
    for k in range(TOP_K):
        acc = acc + gt[:, k:k + 1] * _load_token_tiles(buf, TC, lead=(cur, k), pitch=GATHER_PITCH)
    out_ref[...] = _rms(acc, fg_ref[...])


def _full(shape):
    n = len(shape)
    return pl.BlockSpec(shape, lambda *_: (0,) * n)


def kernel(x, norm1_g, w_in, b_in, conv_w, conv_b, conv_ln_g, conv_ln_b, sg_ln_g, sg_ln_b,
           sg_w, sg_b, grp_g_conv, grp_g_sg, w_out, b_out, norm2_g, w_router, b_router,
           w_exp1, b_exp1, w_exp2, b_exp2, final_g):
    B, S, D = x.shape
    assert D == D_MODEL and S % MIX_TILE == 0 and norm1_g.shape[0] == 1
    N = B * S
    T = MIX_TILE
    n_s = S // T
    l = 0

    row = lambda v: v.reshape(1, -1).astype(F32)
    cw = jnp.repeat(conv_w[l].astype(F32), SUBLANES, axis=0)
    sgw = sg_w[l].reshape(GMLP_HEADS // 2, 2, GMLP_BLOCK, GMLP_BLOCK)
    sgw = sgw.transpose(0, 2, 1, 3).reshape(GMLP_HEADS // 2, GMLP_BLOCK, 2 * GMLP_BLOCK)
    sgb = jnp.repeat(sg_b[l].T, GMLP_HEAD_DIM, axis=1)
    wr_t = w_router[l].T
    br = b_router[l].reshape(N_EXPERTS, 1)

    mix_in = [
        x, row(norm1_g[l]), w_in[l].astype(BF16), row(b_in[l]), cw, row(conv_b[l]),
        row(conv_ln_g[l]), row(conv_ln_b[l]), row(sg_ln_g[l]), row(sg_ln_b[l]), sgw, sgb,
        row(grp_g_conv[l]), row(grp_g_sg[l]), w_out[l].astype(BF16), row(b_out[l]),
        row(norm2_g[l]), wr_t, br,
    ]
    mix_specs = [pl.BlockSpec((None, T, D), lambda b, s: (b, s, 0))]
    mix_specs += [_full(a.shape) for a in mix_in[1:]]
    tok_spec = pl.BlockSpec((TOP_K, T), lambda b, s: (0, b * n_s + s))
    x1, h2, idx, gate, rank, cnt = pl.pallas_call(
        _mix_kernel,
        grid=(B, n_s),
        in_specs=mix_specs,
        out_specs=[
            pl.BlockSpec((None, T, D), lambda b, s: (b, s, 0)),
            pl.BlockSpec((T * TOKEN_LINES, LANES), lambda b, s: (b * n_s + s, 0)),
            tok_spec, tok_spec, tok_spec,
            pl.BlockSpec((N_EXPERTS, 128), lambda b, s: (0, 0)),
        ],
        out_shape=[
            jax.ShapeDtypeStruct((B, S, D), F32),
            jax.ShapeDtypeStruct((N * TOKEN_LINES, LANES), F32),
            jax.ShapeDtypeStruct((TOP_K, N), jnp.int32),
            jax.ShapeDtypeStruct((TOP_K, N), F32),
            jax.ShapeDtypeStruct((TOP_K, N), jnp.int32),
            jax.ShapeDtypeStruct((N_EXPERTS, 128), jnp.int32),
        ],
        scratch_shapes=[
            pltpu.VMEM((T + CONV_HIST, D_CONV), F32),
            pltpu.VMEM((SUBLANES - 1, T + CONV_HIST, D_CONV), F32),
            pltpu.VMEM((T, D_MODEL), BF16),
            pltpu.VMEM((N_EXPERTS, 128), F32),
        ],
        compiler_params=pltpu.CompilerParams(
            dimension_semantics=("arbitrary", "arbitrary"), vmem_limit_bytes=VMEM_LIMIT),
        name="mix",
    )(*mix_in)

    blk = EXPERT_BLOCK
    n_blocks = (N * TOP_K) // blk + N_EXPERTS
    n_slots = n_blocks * blk
    counts = cnt[:, 0]
    nblk_e = (counts + blk - 1) // blk
    block_end = jnp.cumsum(nblk_e)
    pad_start = (block_end - nblk_e) * blk
    dest = rank
    for e in range(N_EXPERTS):
        dest = dest + jnp.where(idx == e, pad_start[e], 0)
    dest = dest * TOKEN_LINES
    blk_ids = jnp.arange(n_blocks, dtype=jnp.int32)
    block_e = jnp.minimum(
        jnp.sum((blk_ids[:, None] >= block_end[None, :]).astype(jnp.int32), axis=1),
        N_EXPERTS - 1).astype(jnp.int32)
    total = block_end[-1:].astype(jnp.int32)
    pad_lo = ((pad_start + counts) * TOKEN_LINES).astype(jnp.int32)
    pad_n = (nblk_e * blk - counts).astype(jnp.int32)
    assert N // DISPATCH_TILE >= N_EXPERTS

    xs = pl.pallas_call(
        _dispatch_kernel,
        grid=(N // DISPATCH_TILE,),
        in_specs=[
            pl.BlockSpec((TOP_K, DISPATCH_TILE), lambda i: (0, i), memory_space=pltpu.SMEM),
            pl.BlockSpec(memory_space=pltpu.SMEM),
            pl.BlockSpec(memory_space=pltpu.SMEM),
            pl.BlockSpec(memory_space=pltpu.SMEM),
            pl.BlockSpec((DISPATCH_TILE * TOKEN_LINES, LANES), lambda i: (i, 0)),
        ],
        out_specs=pl.BlockSpec(memory_space=pl.ANY),
        out_shape=jax.ShapeDtypeStruct((n_slots * TOKEN_LINES, LANES), F32),
        scratch_shapes=[pltpu.VMEM((blk * TOKEN_LINES, LANES), F32),
                        pltpu.SemaphoreType.DMA(()), pltpu.SemaphoreType.DMA(()),
                        pltpu.SemaphoreType.DMA(())],
        compiler_params=pltpu.CompilerParams(
            dimension_semantics=("arbitrary",), vmem_limit_bytes=VMEM_LIMIT),
        name="dispatch",
    )(dest, pad_lo, pad_n, total, h2)

    e_ids = jnp.arange(N_EXPERTS, dtype=jnp.int32)[None, :]
    owner = block_e[:, None] == e_ids
    run_end = jnp.sum(jnp.where(owner, block_end[None, :], 0), axis=1)
    next_owner = jnp.sum((run_end[:, None] >= block_end[None, :]).astype(jnp.int32), axis=1)
    next_e = jnp.where(run_end < total[0], next_owner, -1).astype(jnp.int32)
    nonempty = (nblk_e > 0).astype(jnp.int32)
    run_rank = jnp.sum(jnp.where(e_ids < block_e[:, None], nonempty[None, :], 0), axis=1)
    w_slot = (run_rank % 2).astype(jnp.int32)

    def blk_map(i, be, tot, nxt, slot):
        return (jnp.minimum(i, tot[0] - 1), 0)

    def exp_map(i, be, tot, nxt, slot):
        return (be[jnp.minimum(i, tot[0] - 1)], 0, 0)

    y = pl.pallas_call(
        _expert_kernel,
        grid_spec=pltpu.PrefetchScalarGridSpec(
            num_scalar_prefetch=4,
            grid=(n_blocks,),
            in_specs=[
                pl.BlockSpec((blk * TOKEN_LINES, LANES), blk_map),
                pl.BlockSpec(memory_space=pl.ANY),
                pl.BlockSpec((None, 1, 2 * D_FF), exp_map),
                pl.BlockSpec(memory_space=pl.ANY),
                pl.BlockSpec((None, 1, D), exp_map),
            ],
            out_specs=pl.BlockSpec((blk * TOKEN_LINES, LANES), lambda i, be, tot, nxt, slot: (i, 0)),
            scratch_shapes=[pltpu.VMEM((2, D, 2 * D_FF), F32), pltpu.VMEM((2, D_FF, D), F32),
                            pltpu.SemaphoreType.DMA((2,)), pltpu.SemaphoreType.DMA((2,))],
        ),
        out_shape=jax.ShapeDtypeStruct((n_slots * TOKEN_LINES, LANES), F32),
        compiler_params=pltpu.CompilerParams(
            dimension_semantics=("arbitrary",), vmem_limit_bytes=VMEM_LIMIT),
        name="experts",
    )(block_e, total, next_e, w_slot, xs, w_exp1[l], b_exp1[l].reshape(N_EXPERTS, 1, 2 * D_FF),
      w_exp2[l], b_exp2[l].reshape(N_EXPERTS, 1, D))

    TC = COMBINE_TILE
    out = pl.pallas_call(
        _combine_kernel,
        grid=(N // TC,),
        in_specs=[
            pl.BlockSpec((TOP_K, TC), lambda i: (0, i), memory_space=pltpu.SMEM),
            pl.BlockSpec((TOP_K, TC), lambda i: (0, jnp.minimum(i + 1, N // TC - 1)),
                         memory_space=pltpu.SMEM),
            pl.BlockSpec((TC, D), lambda i: (i, 0)),
            pl.BlockSpec((TOP_K, TC), lambda i: (0, i)),
            _full((1, D)),
            pl.BlockSpec(memory_space=pl.ANY),
        ],
        out_specs=pl.BlockSpec((TC, D), lambda i: (i, 0)),
        out_shape=jax.ShapeDtypeStruct((N, D), F32),
        scratch_shapes=[pltpu.VMEM((2, TOP_K, TC * GATHER_PITCH, LANES), F32),
                        pltpu.SemaphoreType.DMA((2,))],
        compiler_params=pltpu.CompilerParams(
            dimension_semantics=("arbitrary",), vmem_limit_bytes=VMEM_LIMIT),
        name="combine",
    )(dest, dest, x1.reshape(N, D), gate, row(final_g), y)
    return out.reshape(B, S, D)
```

```python
import jax
import jax.numpy as jnp
from jax import lax
from jax.experimental import pallas as pl
from jax.experimental.pallas import tpu as pltpu

D_MODEL = 1024
D_CONV = 512
D_GMLP = 512
CONV_WIDTH = 31
CONV_HIST = 32
GMLP_HEADS = 8
GMLP_HEAD_DIM = 64
GMLP_BLOCK = 128
CHUNK = 64
N_EXPERTS = 32
TOP_K = 4
D_FF = 1024
SWIGLU_ALPHA = 1.702
SWIGLU_LIMIT = 7.0
EPS = 1e-5

MIX_TILE = 512
CONV_ROWS = 32
SUBLANES = 8
EXPERT_BLOCK = 512
DISPATCH_TILE = 1024
COMBINE_TILE = 512
ISSUE_UNROLL = 8
PAD_BITS = tuple(1 << b for b in reversed(range(EXPERT_BLOCK.bit_length() - 1)))
VMEM_LIMIT = 56 * 1024 * 1024

F32 = jnp.float32
BF16 = jnp.bfloat16


def _rms(x, g):
    return x * lax.rsqrt(jnp.mean(x * x, axis=-1, keepdims=True) + EPS) * g


def _ln(x, g, b):
    mu = jnp.mean(x, axis=-1, keepdims=True)
    xc = x - mu
    var = jnp.mean(xc * xc, axis=-1, keepdims=True)
    return xc * lax.rsqrt(var + EPS) * g + b


def _gelu(x):
    return 0.5 * x * (1.0 + lax.erf(x * (2.0 ** -0.5)))


LANES = 128
TOKEN_LINES = D_MODEL // LANES


GATHER_PITCH = TOKEN_LINES + 1


def _load_token_tiles(ref, rows, lead=(), pitch=TOKEN_LINES):
    chunks = [ref[lead + (pl.ds(c, rows, stride=pitch), slice(None))] for c in range(TOKEN_LINES)]
    return jnp.concatenate(chunks, axis=1)


def _store_token_tiles(ref, value, rows):
    for c in range(TOKEN_LINES):
        ref[pl.ds(c, rows, stride=TOKEN_LINES), :] = value[:, c * LANES:(c + 1) * LANES]


def _mix_kernel(x_ref, g1_ref, win_ref, bin_ref, cw_ref, cb_ref, clg_ref, clb_ref,
                slg_ref, slb_ref, sgw_ref, sgb_ref, ggc_ref, ggs_ref, wout_ref, bout_ref,
                g2_ref, wr_ref, br_ref,
                x1_ref, h2_ref, idx_ref, gate_ref, rank_ref, cnt_ref,
                ubuf, ushift, ycat, cnt_s):
    T = MIX_TILE
    b = pl.program_id(0)
    s = pl.program_id(1)

    @pl.when(s == 0)
    def _():
        ubuf[0:CONV_HIST, :] = jnp.zeros((CONV_HIST, D_CONV), F32)

    @pl.when((b == 0) & (s == 0))
    def _():
        cnt_s[...] = jnp.zeros_like(cnt_s)

    x = x_ref[...]
    h = _rms(x, g1_ref[...])
    p = jnp.dot(h.astype(BF16), win_ref[...], preferred_element_type=F32) + bin_ref[...]

    u = p[:, 0:D_CONV] * jax.nn.sigmoid(p[:, D_CONV:2 * D_CONV])
    ubuf[CONV_HIST:CONV_HIST + T, :] = u

    n_rows = T + CONV_HIST
    for r in range(1, SUBLANES):
        ushift[r - 1] = pltpu.roll(ubuf[...], n_rows - r, axis=0)

    for c in range(T // CONV_ROWS):
        r0 = c * CONV_ROWS
        acc = jnp.broadcast_to(cb_ref[...], (CONV_ROWS, D_CONV))
        for k in range(CONV_WIDTH):
            off = CONV_HIST - (CONV_WIDTH - 1) + k
            q, r = divmod(off, SUBLANES)
            a0 = r0 + q * SUBLANES
            if r == 0:
                tap = ubuf[a0:a0 + CONV_ROWS, :]
            else:
                tap = ushift[r - 1, a0:a0 + CONV_ROWS, :]
            w8 = cw_ref[k * SUBLANES:(k + 1) * SUBLANES, :]
            acc = acc + jnp.concatenate([w8] * (CONV_ROWS // SUBLANES), axis=0) * tap
        y = _ln(acc, clg_ref[...], clb_ref[...])
        y = y * jax.nn.sigmoid(y)
        y = _rms(y, ggc_ref[...])
        ycat[r0:r0 + CONV_ROWS, 0:D_CONV] = y.astype(BF16)
    ubuf[0:CONV_HIST, :] = ubuf[T:T + CONV_HIST, :]

    uu = _gelu(p[:, 2 * D_CONV:2 * D_CONV + D_GMLP])
    vv = _ln(_gelu(p[:, 2 * D_CONV + D_GMLP:]), slg_ref[...], slb_ref[...])
    t_out = lax.broadcasted_iota(jnp.int32, (GMLP_BLOCK, 2 * GMLP_BLOCK), 0)
    s_in = lax.broadcasted_iota(jnp.int32, (GMLP_BLOCK, 2 * GMLP_BLOCK), 1) % GMLP_BLOCK
    wmask = (s_in // CHUNK) <= (t_out // CHUNK)
    lane = lax.broadcasted_iota(jnp.int32, (GMLP_BLOCK, 2 * GMLP_HEAD_DIM), 1)
    lo = lane < GMLP_HEAD_DIM
    n_pair = GMLP_HEADS // 2
    wcat = [jnp.where(wmask, sgw_ref[j], 0.0).astype(BF16) for j in range(n_pair)]
    for blk in range(T // GMLP_BLOCK):
        rows = slice(blk * GMLP_BLOCK, (blk + 1) * GMLP_BLOCK)
        outs = []
        for j in range(n_pair):
            cols = slice(j * 128, (j + 1) * 128)
            vp = vv[rows, cols]
            vstack = jnp.concatenate(
                [jnp.where(lo, vp, 0.0), jnp.where(lo, 0.0, vp)], axis=0).astype(BF16)
            mixed = jnp.dot(wcat[j], vstack, preferred_element_type=F32) + sgb_ref[:, cols]
            outs.append(uu[rows, cols] * mixed)
        ysg = jnp.concatenate(outs, axis=1)
        ycat[rows, D_CONV:] = _rms(ysg, ggs_ref[...]).astype(BF16)

    o = jnp.dot(ycat[...], wout_ref[...], preferred_element_type=F32) + bout_ref[...]
    x1 = x + o
    x1_ref[...] = x1
    h2 = _rms(x1, g2_ref[...])
    _store_token_tiles(h2_ref, h2, T)

    logits = lax.dot_general(wr_ref[...].astype(BF16), h2.astype(BF16),
                             (((1,), (1,)), ((), ())),
                             preferred_element_type=F32) + br_ref[...]
    e_iota = lax.broadcasted_iota(jnp.int32, (N_EXPERTS, T), 0)
    work = logits
    vals, idxs, sels = [], [], []
    for _k in range(TOP_K):
        m = jnp.max(work, axis=0, keepdims=True)
        am = jnp.min(jnp.where(work == m, e_iota, N_EXPERTS), axis=0, keepdims=True)
        sel = e_iota == am
        vals.append(m)
        idxs.append(am)
        sels.append(sel)
        work = jnp.where(sel, -jnp.inf, work)
    exps = [jnp.exp(v - vals[0]) for v in vals]
    denom = exps[0] + exps[1] + exps[2] + exps[3]
    gate_ref[...] = jnp.concatenate([e / denom for e in exps], axis=0)
    idx_ref[...] = jnp.concatenate(idxs, axis=0)

    member = jnp.zeros((N_EXPERTS, T), F32)
    for sel in sels:
        member = member + jnp.where(sel, 1.0, 0.0)
    r_i = lax.broadcasted_iota(jnp.int32, (T, T), 0)
    c_i = lax.broadcasted_iota(jnp.int32, (T, T), 1)
    upper = jnp.where(r_i < c_i, 1.0, 0.0).astype(BF16)
    before = jnp.dot(member.astype(BF16), upper, preferred_element_type=F32)
    before = before + cnt_s[:, 0:1]
    ranks = [jnp.sum(jnp.where(sel, before, 0.0), axis=0, keepdims=True) for sel in sels]
    rank_ref[...] = jnp.concatenate(ranks, axis=0).astype(jnp.int32)
    cnt_new = cnt_s[...] + jnp.sum(member, axis=1, keepdims=True)
    cnt_s[...] = cnt_new
    cnt_ref[...] = cnt_new.astype(jnp.int32)


def _dispatch_kernel(dest_ref, padlo_ref, padn_ref, tot_ref, h2_ref, xs_ref, zbuf, sem, zsem, tsem):
    i = pl.program_id(0)
    blk_lines = EXPERT_BLOCK * TOKEN_LINES
    n_blocks = xs_ref.shape[0] // blk_lines

    def pad_pieces(e, op):
        n = padn_ref[e]
        line = padlo_ref[e]
        for bit in PAD_BITS:
            hit = (n & bit) != 0

            @pl.when(hit)
            def _(line=line, bit=bit):
                op(pltpu.make_async_copy(
                    zbuf.at[pl.ds(0, bit * TOKEN_LINES)],
                    xs_ref.at[pl.ds(pl.multiple_of(line, TOKEN_LINES), bit * TOKEN_LINES)], zsem))

            line = line + jnp.where(hit, bit * TOKEN_LINES, 0)

    def for_each_expert(op):
        def body(e, carry):
            pad_pieces(e, op)
            return carry
        lax.fori_loop(0, N_EXPERTS, body, 0)

    @pl.when(i == 0)
    def _():
        zbuf[...] = jnp.zeros_like(zbuf)
        for_each_expert(lambda c: c.start())

    tail_blk = tot_ref[0] + i

    def tail_copy():
        line = pl.multiple_of(tail_blk * blk_lines, blk_lines)
        return pltpu.make_async_copy(zbuf, xs_ref.at[pl.ds(line, blk_lines)], tsem)

    @pl.when(tail_blk < n_blocks)
    def _():
        tail_copy().start()

    def start(g, carry):
        for j in range(ISSUE_UNROLL):
            t = g * ISSUE_UNROLL + j
            src = h2_ref.at[pl.ds(pl.multiple_of(t * TOKEN_LINES, TOKEN_LINES), TOKEN_LINES)]
            for k in range(TOP_K):
                line = pl.multiple_of(dest_ref[k, t], TOKEN_LINES)
                pltpu.make_async_copy(src, xs_ref.at[pl.ds(line, TOKEN_LINES)], sem).start(priority=k % 2)
        return carry

    lax.fori_loop(0, DISPATCH_TILE // ISSUE_UNROLL, start, 0)
    for k in range(TOP_K):
        pltpu.make_async_copy(h2_ref, xs_ref.at[pl.ds(0, DISPATCH_TILE * TOKEN_LINES)], sem).wait()

    @pl.when(i == 0)
    def _():
        for_each_expert(lambda c: c.wait())

    @pl.when(tail_blk < n_blocks)
    def _():
        tail_copy().wait()


def _expert_kernel(be_ref, tot_ref, nxt_ref, slot_ref, xs_ref, w1_hbm, b1_ref, w2_hbm, b2_ref, y_ref,
                   w1buf, w2buf, wsem1, wsem2):
    i = pl.program_id(0)

    def fetch(e, s):
        return (pltpu.make_async_copy(w1_hbm.at[e], w1buf.at[s], wsem1.at[s]),
                pltpu.make_async_copy(w2_hbm.at[e], w2buf.at[s], wsem2.at[s]))

    @pl.when(i < tot_ref[0])
    def _():
        e = be_ref[i]
        s = slot_ref[i]

        @pl.when(i == 0)
        def _():
            for cp in fetch(e, s):
                cp.start()

        @pl.when((i == 0) | (e != be_ref[jnp.maximum(i - 1, 0)]))
        def _():
            for cp in fetch(e, s):
                cp.wait()

            @pl.when(nxt_ref[i] >= 0)
            def _():
                for cp in fetch(nxt_ref[i], 1 - s):
                    cp.start()

        xb = _load_token_tiles(xs_ref, EXPERT_BLOCK).astype(BF16)
        hcat = lax.dot_general(xb, w1buf[s], (((1,), (0,)), ((), ())),
                               preferred_element_type=F32) + b1_ref[...]
        x_glu = jnp.minimum(hcat[:, :D_FF], SWIGLU_LIMIT)
        x_lin = jnp.clip(hcat[:, D_FF:], -SWIGLU_LIMIT, SWIGLU_LIMIT)
        act = x_glu * jax.nn.sigmoid(SWIGLU_ALPHA * x_glu) * (x_lin + 1.0)
        y = lax.dot_general(act.astype(BF16), w2buf[s], (((1,), (0,)), ((), ())),
                            preferred_element_type=F32) + b2_ref[...]
        _store_token_tiles(y_ref, y, EXPERT_BLOCK)

    @pl.when(i >= tot_ref[0])
    def _():
        y_ref[...] = jnp.zeros_like(y_ref)


def _combine_kernel(dest_ref, dnext_ref, x1_ref, gate_ref, fg_ref, y_ref, out_ref, buf, sem):
    TC = COMBINE_TILE
    i = pl.program_id(0)
    cur = i % 2
    nxt = (i + 1) % 2

    def issue(d_ref, slot):
        def start(g, carry):
            for j in range(ISSUE_UNROLL):
                t = g * ISSUE_UNROLL + j
                row = t * GATHER_PITCH
                for k in range(TOP_K):
                    line = pl.multiple_of(d_ref[k, t], TOKEN_LINES)
                    pltpu.make_async_copy(
                        y_ref.at[pl.ds(line, TOKEN_LINES)],
                        buf.at[slot, k, pl.ds(row, TOKEN_LINES)],
                        sem.at[slot]).start(priority=k % 2)
            return carry

        lax.fori_loop(0, TC // ISSUE_UNROLL, start, 0)

    @pl.when(i == 0)
    def _():
        issue(dest_ref, cur)

    @pl.when(i + 1 < pl.num_programs(0))
    def _():
        issue(dnext_ref, nxt)

    for k in range(TOP_K):
        pltpu.make_async_copy(y_ref.at[pl.ds(0, TC * TOKEN_LINES)],
                              buf.at[cur, k, pl.ds(0, TC * TOKEN_LINES)], sem.at[cur]).wait()

    gpad = jnp.concatenate([gate_ref[...], jnp.zeros((128 - TOP_K, TC), F32)], axis=0)
    gt = gpad.T
    acc = x1_ref[...]
    for k in range(TOP_K):
        acc = acc + gt[:, k:k + 1] * _load_token_tiles(buf, TC, lead=(cur, k), pitch=GATHER_PITCH)
    out_ref[...] = _rms(acc, fg_ref[...])


def _full(shape):
    n = len(shape)
    return pl.BlockSpec(shape, lambda *_: (0,) * n)


def kernel(x, norm1_g, w_in, b_in, conv_w, conv_b, conv_ln_g, conv_ln_b, sg_ln_g, sg_ln_b,
           sg_w, sg_b, grp_g_conv, grp_g_sg, w_out, b_out, norm2_g, w_router, b_router,
           w_exp1, b_exp1, w_exp2, b_exp2, final_g):
    B, S, D = x.shape
    assert D == D_MODEL and S % MIX_TILE == 0 and norm1_g.shape[0] == 1
    N = B * S
    T = MIX_TILE
    n_s = S // T
    l = 0

    row = lambda v: v.reshape(1, -1).astype(F32)
    cw = jnp.repeat(conv_w[l].astype(F32), SUBLANES, axis=0)
    sgw = sg_w[l].reshape(GMLP_HEADS // 2, 2, GMLP_BLOCK, GMLP_BLOCK)
    sgw = sgw.transpose(0, 2, 1, 3).reshape(GMLP_HEADS // 2, GMLP_BLOCK, 2 * GMLP_BLOCK)
    sgb = jnp.repeat(sg_b[l].T, GMLP_HEAD_DIM, axis=1)
    wr_t = w_router[l].T
    br = b_router[l].reshape(N_EXPERTS, 1)

    mix_in = [
        x, row(norm1_g[l]), w_in[l].astype(BF16), row(b_in[l]), cw, row(conv_b[l]),
        row(conv_ln_g[l]), row(conv_ln_b[l]), row(sg_ln_g[l]), row(sg_ln_b[l]), sgw, sgb,
        row(grp_g_conv[l]), row(grp_g_sg[l]), w_out[l].astype(BF16), row(b_out[l]),
        row(norm2_g[l]), wr_t, br,
    ]
    mix_specs = [pl.BlockSpec((None, T, D), lambda b, s: (b, s, 0))]
    mix_specs += [_full(a.shape) for a in mix_in[1:]]
    tok_spec = pl.BlockSpec((TOP_K, T), lambda b, s: (0, b * n_s + s))
    x1, h2, idx, gate, rank, cnt = pl.pallas_call(
        _mix_kernel,
        grid=(B, n_s),
        in_specs=mix_specs,
        out_specs=[
            pl.BlockSpec((None, T, D), lambda b, s: (b, s, 0)),
            pl.BlockSpec((T * TOKEN_LINES, LANES), lambda b, s: (b * n_s + s, 0)),
            tok_spec, tok_spec, tok_spec,
            pl.BlockSpec((N_EXPERTS, 128), lambda b, s: (0, 0)),
        ],
        out_shape=[
            jax.ShapeDtypeStruct((B, S, D), F32),
            jax.ShapeDtypeStruct((N * TOKEN_LINES, LANES), F32),
            jax.ShapeDtypeStruct((TOP_K, N), jnp.int32),
            jax.ShapeDtypeStruct((TOP_K, N), F32),
            jax.ShapeDtypeStruct((TOP_K, N), jnp.int32),
            jax.ShapeDtypeStruct((N_EXPERTS, 128), jnp.int32),
        ],
        scratch_shapes=[
            pltpu.VMEM((T + CONV_HIST, D_CONV), F32),
            pltpu.VMEM((SUBLANES - 1, T + CONV_HIST, D_CONV), F32),
            pltpu.VMEM((T, D_MODEL), BF16),
            pltpu.VMEM((N_EXPERTS, 128), F32),
        ],
        compiler_params=pltpu.CompilerParams(
            dimension_semantics=("arbitrary", "arbitrary"), vmem_limit_bytes=VMEM_LIMIT),
        name="mix",
    )(*mix_in)

    blk = EXPERT_BLOCK
    n_blocks = (N * TOP_K) // blk + N_EXPERTS
    n_slots = n_blocks * blk
    counts = cnt[:, 0]
    nblk_e = (counts + blk - 1) // blk
    block_end = jnp.cumsum(nblk_e)
    pad_start = (block_end - nblk_e) * blk
    dest = rank
    for e in range(N_EXPERTS):
        dest = dest + jnp.where(idx == e, pad_start[e], 0)
    dest = dest * TOKEN_LINES
    blk_ids = jnp.arange(n_blocks, dtype=jnp.int32)
    block_e = jnp.minimum(
        jnp.sum((blk_ids[:, None] >= block_end[None, :]).astype(jnp.int32), axis=1),
        N_EXPERTS - 1).astype(jnp.int32)
    total = block_end[-1:].astype(jnp.int32)
    pad_lo = ((pad_start + counts) * TOKEN_LINES).astype(jnp.int32)
    pad_n = (nblk_e * blk - counts).astype(jnp.int32)
    assert N // DISPATCH_TILE >= N_EXPERTS

    xs = pl.pallas_call(
        _dispatch_kernel,
        grid=(N // DISPATCH_TILE,),
        in_specs=[
            pl.BlockSpec((TOP_K, DISPATCH_TILE), lambda i: (0, i), memory_space=pltpu.SMEM),
            pl.BlockSpec(memory_space=pltpu.SMEM),
            pl.BlockSpec(memory_space=pltpu.SMEM),
            pl.BlockSpec(memory_space=pltpu.SMEM),
            pl.BlockSpec((DISPATCH_TILE * TOKEN_LINES, LANES), lambda i: (i, 0)),
        ],
        out_specs=pl.BlockSpec(memory_space=pl.ANY),
        out_shape=jax.ShapeDtypeStruct((n_slots * TOKEN_LINES, LANES), F32),
        scratch_shapes=[pltpu.VMEM((blk * TOKEN_LINES, LANES), F32),
                        pltpu.SemaphoreType.DMA(()), pltpu.SemaphoreType.DMA(()),
                        pltpu.SemaphoreType.DMA(())],
        compiler_params=pltpu.CompilerParams(
            dimension_semantics=("arbitrary",), vmem_limit_bytes=VMEM_LIMIT),
        name="dispatch",
    )(dest, pad_lo, pad_n, total, h2)

    e_ids = jnp.arange(N_EXPERTS, dtype=jnp.int32)[None, :]
    owner = block_e[:, None] == e_ids
    run_end = jnp.sum(jnp.where(owner, block_end[None, :], 0), axis=1)
    next_owner = jnp.sum((run_end[:, None] >= block_end[None, :]).astype(jnp.int32), axis=1)
    next_e = jnp.where(run_end < total[0], next_owner, -1).astype(jnp.int32)
    nonempty = (nblk_e > 0).astype(jnp.int32)
    run_rank = jnp.sum(jnp.where(e_ids < block_e[:, None], nonempty[None, :], 0), axis=1)
    w_slot = (run_rank % 2).astype(jnp.int32)

    def blk_map(i, be, tot, nxt, slot):
        return (jnp.minimum(i, tot[0] - 1), 0)

    def exp_map(i, be, tot, nxt, slot):
        return (be[jnp.minimum(i, tot[0] - 1)], 0, 0)

    y = pl.pallas_call(
        _expert_kernel,
        grid_spec=pltpu.PrefetchScalarGridSpec(
            num_scalar_prefetch=4,
            grid=(n_blocks,),
            in_specs=[
                pl.BlockSpec((blk * TOKEN_LINES, LANES), blk_map),
                pl.BlockSpec(memory_space=pl.ANY),
                pl.BlockSpec((None, 1, 2 * D_FF), exp_map),
                pl.BlockSpec(memory_space=pl.ANY),
                pl.BlockSpec((None, 1, D), exp_map),
            ],
            out_specs=pl.BlockSpec((blk * TOKEN_LINES, LANES), lambda i, be, tot, nxt, slot: (i, 0)),
            scratch_shapes=[pltpu.VMEM((2, D, 2 * D_FF), F32), pltpu.VMEM((2, D_FF, D), F32),
                            pltpu.SemaphoreType.DMA((2,)), pltpu.SemaphoreType.DMA((2,))],
        ),
        out_shape=jax.ShapeDtypeStruct((n_slots * TOKEN_LINES, LANES), F32),
        compiler_params=pltpu.CompilerParams(
            dimension_semantics=("arbitrary",), vmem_limit_bytes=VMEM_LIMIT),
        name="experts",
    )(block_e, total, next_e, w_slot, xs, w_exp1[l], b_exp1[l].reshape(N_EXPERTS, 1, 2 * D_FF),
      w_exp2[l], b_exp2[l].reshape(N_EXPERTS, 1, D))

    TC = COMBINE_TILE
    out = pl.pallas_call(
        _combine_kernel,
        grid=(N // TC,),
        in_specs=[
            pl.BlockSpec((TOP_K, TC), lambda i: (0, i), memory_space=pltpu.SMEM),
            pl.BlockSpec((TOP_K, TC), lambda i: (0, jnp.minimum(i + 1, N // TC - 1)),
                         memory_space=pltpu.SMEM),
            pl.BlockSpec((TC, D), lambda i: (i, 0)),
            pl.BlockSpec((TOP_K, TC), lambda i: (0, i)),
            _full((1, D)),
            pl.BlockSpec(memory_space=pl.ANY),
        ],
        out_specs=pl.BlockSpec((TC, D), lambda i: (i, 0)),
        out_shape=jax.ShapeDtypeStruct((N, D), F32),
        scratch_shapes=[pltpu.VMEM((2, TOP_K, TC * GATHER_PITCH, LANES), F32),
                        pltpu.SemaphoreType.DMA((2,))],
        compiler_params=pltpu.CompilerParams(
            dimension_semantics=("arbitrary",), vmem_limit_bytes=VMEM_LIMIT),
        name="combine",
    )(dest, dest, x1.reshape(N, D), gate, row(final_g), y)
    return out.reshape(B, S, D)
```
